```python
import math
import jax, jax.numpy as jnp
from jax import lax
import numpy as np

D_MODEL = 1024
BATCH = 4
SEQ = 8192
DEPTH = 4

SSM_GROUP = 16
D_SSM = D_MODEL // 2
SSM_GROUPS = D_SSM // SSM_GROUP
SSM_STATE = 64
DT_MIN = 0.001
DT_MAX = 0.1
D_CONV = D_MODEL // 2
CONV_WIDTH = 31
ATTN_HEADS = 8
ATTN_HEAD_DIM = 64
ATTN_MAPS = 2 * ATTN_HEADS
D_QK = ATTN_MAPS * ATTN_HEAD_DIM
D_V = ATTN_HEADS * 2 * ATTN_HEAD_DIM
Q_BLOCK = 128
REL_BUCKETS = 32
REL_MAX_DIST = 128
D_FF = -(-8 * D_MODEL // (3 * 256)) * 256
N_BRANCH = 3
OFF_SSM = 0
OFF_CONV = OFF_SSM + D_SSM
OFF_Q = OFF_CONV + 2 * D_CONV
OFF_K = OFF_Q + D_QK
OFF_V = OFF_K + D_QK
OFF_GATE = OFF_V + D_V
D_IN = OFF_GATE + N_BRANCH * D_MODEL

RMS_EPS = 1e-6
SUBLN_EPS = 1e-5
LN_EPS = 1e-5

kernel_name = "hybrid_s5_conformer_diffattn_block"


def rms_norm(x, g, eps=RMS_EPS):
    xf = x.astype(jnp.float32)
    y = xf * lax.rsqrt(jnp.mean(xf * xf, axis=-1, keepdims=True) + eps)
    return (y * g.astype(jnp.float32)).astype(x.dtype)


def layer_norm(x, g, b, eps=LN_EPS):
    xf = x.astype(jnp.float32)
    mu = jnp.mean(xf, axis=-1, keepdims=True)
    var = jnp.mean(jnp.square(xf - mu), axis=-1, keepdims=True)
    y = (xf - mu) * lax.rsqrt(var + eps)
    return (y * g.astype(jnp.float32) + b.astype(jnp.float32)).astype(x.dtype)


def _scan_op(e1, e2):
    a1, b1 = e1
    a2, b2 = e2
    return a1 * a2, a2 * b1 + b2


def s5_branch(u, a_re, a_im, log_dt, b_re, b_im, c_re, c_im, d_skip, w_glu):
    bsz, length, _ = u.shape
    uf = u.astype(jnp.float32).reshape(bsz, length, SSM_GROUPS, SSM_GROUP)
    lam = lax.complex(a_re.astype(jnp.float32), a_im.astype(jnp.float32))
    dt = jnp.exp(log_dt.astype(jnp.float32))[:, None]
    lam_bar = jnp.exp(lam * dt)
    b_mat = lax.complex(b_re.astype(jnp.float32), b_im.astype(jnp.float32))
    b_bar = ((lam_bar - 1.0) / lam)[..., None] * b_mat
    bu = jnp.einsum("blgc,gpc->blgp", uf.astype(jnp.complex64), b_bar)
    a = jnp.broadcast_to(lam_bar, bu.shape)
    _, states = lax.associative_scan(_scan_op, (a, bu), axis=1)
    c_mat = lax.complex(c_re.astype(jnp.float32), c_im.astype(jnp.float32))
    y = jnp.real(jnp.einsum("blgp,gcp->blgc", states, c_mat)) + d_skip.astype(jnp.float32) * uf
    y = jax.nn.gelu(y.reshape(bsz, length, D_SSM)).astype(u.dtype)
    z = y @ w_glu
    return z[..., :D_SSM] * jax.nn.sigmoid(z[..., D_SSM:])


def conformer_conv(c_in, dw, dw_b, ln_g, ln_b, w_out):
    a, b = jnp.split(c_in, 2, axis=-1)
    g = a * jax.nn.sigmoid(b)
    y = lax.conv_general_dilated(
        g, dw[:, None, :], window_strides=(1,), padding=[(CONV_WIDTH - 1, 0)],
        dimension_numbers=("NWC", "WIO", "NWC"), feature_group_count=D_CONV) + dw_b
    y = jax.nn.silu(layer_norm(y, ln_g, ln_b))
    return y @ w_out


def rel_bucket(rel):
    n = jnp.maximum(rel, 0)
    max_exact = REL_BUCKETS // 2
    nf = jnp.maximum(n, max_exact).astype(jnp.float32)
    large = max_exact + (jnp.log(nf / max_exact) / math.log(REL_MAX_DIST / max_exact)
                         * (REL_BUCKETS - max_exact)).astype(jnp.int32)
    large = jnp.minimum(large, REL_BUCKETS - 1)
    return jnp.where(n < max_exact, n, large)


def diff_attention(q, k, v, rel_bias, lam, subln_g, lambda_init):
    bsz, length, _ = q.shape
    q = q.reshape(bsz, length, ATTN_MAPS, ATTN_HEAD_DIM)
    k = k.reshape(bsz, length, ATTN_MAPS, ATTN_HEAD_DIM)
    v = v.reshape(bsz, length, ATTN_HEADS, 2 * ATTN_HEAD_DIM)
    n_blk = length // Q_BLOCK
    qb = q.reshape(bsz, n_blk, Q_BLOCK, ATTN_MAPS, ATTN_HEAD_DIM).transpose(1, 0, 2, 3, 4)
    k_pos = jnp.arange(length)
    scale = ATTN_HEAD_DIM ** -0.5

    def one_block(args):
        i, qi = args
        q_pos = i * Q_BLOCK + jnp.arange(Q_BLOCK)
        rel = q_pos[:, None] - k_pos[None, :]
        bias = rel_bias[rel_bucket(rel)].astype(jnp.float32).transpose(2, 0, 1)
        s = jnp.einsum("bqmd,bkmd->bmqk", qi, k).astype(jnp.float32) * scale + bias
        s = jnp.where(rel >= 0, s, -jnp.inf)
        p = jax.nn.softmax(s, axis=-1).reshape(bsz, ATTN_HEADS, 2, Q_BLOCK, length)
        attn = p[:, :, 0] - lam * p[:, :, 1]
        return jnp.einsum("bhqk,bkhe->bqhe", attn.astype(v.dtype), v)

    o = lax.map(one_block, (jnp.arange(n_blk), qb))
    o = o.transpose(1, 0, 2, 3, 4).reshape(bsz, length, ATTN_HEADS, 2 * ATTN_HEAD_DIM)
    o = rms_norm(o, subln_g, SUBLN_EPS) * (1.0 - lambda_init)
    return o.reshape(bsz, length, D_V)


def setup_inputs(seed: int = 0) -> dict:
    key = jax.random.key(seed)
    ks = jax.random.split(key, 32)
    f32 = jnp.float32

    def nrm(k, shape, scale):
        return jax.random.normal(k, shape, f32) * scale

    L = DEPTH
    a_im_base = math.pi * jnp.arange(SSM_STATE, dtype=f32)
    return {
        "x": nrm(ks[0], (BATCH, SEQ, D_MODEL), 1.0),
        "rel_bias": nrm(ks[1], (REL_BUCKETS, ATTN_MAPS), 0.5),
        "pre_mix_g": 1.0 + nrm(ks[2], (L, D_MODEL), 0.01),
        "w_in": nrm(ks[3], (L, D_MODEL, D_IN), D_MODEL ** -0.5),
        "ssm_a_re": -0.5 + nrm(ks[4], (L, SSM_GROUPS, SSM_STATE), 0.01),
        "ssm_a_im": a_im_base + nrm(ks[5], (L, SSM_GROUPS, SSM_STATE), 0.01),
        "ssm_log_dt": jax.random.uniform(ks[6], (L, SSM_GROUPS), f32, math.log(DT_MIN), math.log(DT_MAX)),
        "ssm_b_re": nrm(ks[7], (L, SSM_GROUPS, SSM_STATE, SSM_GROUP), (2 * SSM_GROUP) ** -0.5),
        "ssm_b_im": nrm(ks[8], (L, SSM_GROUPS, SSM_STATE, SSM_GROUP), (2 * SSM_GROUP) ** -0.5),
        "ssm_c_re": nrm(ks[9], (L, SSM_GROUPS, SSM_GROUP, SSM_STATE), (2 * SSM_STATE) ** -0.5),
        "ssm_c_im": nrm(ks[10], (L, SSM_GROUPS, SSM_GROUP, SSM_STATE), (2 * SSM_STATE) ** -0.5),
        "ssm_d": nrm(ks[11], (L, SSM_GROUPS, SSM_GROUP), 1.0),
        "w_ssm_glu": nrm(ks[12], (L, D_SSM, 2 * D_SSM), D_SSM ** -0.5),
        "w_ssm_out": nrm(ks[13], (L, D_SSM, D_MODEL), D_SSM ** -0.5),
        "conv_dw": nrm(ks[14], (L, CONV_WIDTH, D_CONV), CONV_WIDTH ** -0.5),
        "conv_dw_b": nrm(ks[15], (L, D_CONV), 0.01),
        "conv_ln_g": 1.0 + nrm(ks[16], (L, D_CONV), 0.01),
        "conv_ln_b": nrm(ks[17], (L, D_CONV), 0.01),
        "w_conv_out": nrm(ks[18], (L, D_CONV, D_MODEL), D_CONV ** -0.5),
        "lambda_q1": nrm(ks[19], (L, ATTN_HEAD_DIM), 0.1),
        "lambda_k1": nrm(ks[20], (L, ATTN_HEAD_DIM), 0.1),
        "lambda_q2": nrm(ks[21], (L, ATTN_HEAD_DIM), 0.1),
        "lambda_k2": nrm(ks[22], (L, ATTN_HEAD_DIM), 0.1),
        "attn_subln_g": 1.0 + nrm(ks[23], (L, 2 * ATTN_HEAD_DIM), 0.01),
        "w_attn_out": nrm(ks[24], (L, D_V, D_MODEL), D_V ** -0.5),
        "w_out": nrm(ks[25], (L, D_MODEL, D_MODEL), D_MODEL ** -0.5),
        "post_mix_g": 1.0 + nrm(ks[26], (L, D_MODEL), 0.01),
        "pre_ffn_g": 1.0 + nrm(ks[27], (L, D_MODEL), 0.01),
        "w_ffn_in": nrm(ks[28], (L, D_MODEL, 2 * D_FF), D_MODEL ** -0.5),
        "w_ffn_out": nrm(ks[29], (L, D_FF, D_MODEL), D_FF ** -0.5),
        "post_ffn_g": 1.0 + nrm(ks[30], (L, D_MODEL), 0.01),
    }


def reference(x, rel_bias, pre_mix_g, w_in, ssm_a_re, ssm_a_im, ssm_log_dt, ssm_b_re, ssm_b_im,
              ssm_c_re, ssm_c_im, ssm_d, w_ssm_glu, w_ssm_out, conv_dw, conv_dw_b, conv_ln_g,
              conv_ln_b, w_conv_out, lambda_q1, lambda_k1, lambda_q2, lambda_k2, attn_subln_g,
              w_attn_out, w_out, post_mix_g, pre_ffn_g, w_ffn_in, w_ffn_out, post_ffn_g):
    bsz, length, _ = x.shape
    for layer in range(DEPTH):
        lambda_init = 0.8 - 0.6 * math.exp(-0.3 * layer)
        h = rms_norm(x, pre_mix_g[layer])
        proj = h @ w_in[layer]
        u_ssm = proj[..., OFF_SSM:OFF_CONV]
        c_in = proj[..., OFF_CONV:OFF_Q]
        q = proj[..., OFF_Q:OFF_K]
        k = proj[..., OFF_K:OFF_V]
        v = proj[..., OFF_V:OFF_GATE]
        gates = jax.nn.sigmoid(proj[..., OFF_GATE:]).reshape(bsz, length, N_BRANCH, D_MODEL)

        y_a = s5_branch(u_ssm, ssm_a_re[layer], ssm_a_im[layer], ssm_log_dt[layer],
                        ssm_b_re[layer], ssm_b_im[layer], ssm_c_re[layer], ssm_c_im[layer],
                        ssm_d[layer], w_ssm_glu[layer]) @ w_ssm_out[layer]
        y_b = conformer_conv(c_in, conv_dw[layer], conv_dw_b[layer], conv_ln_g[layer],
                             conv_ln_b[layer], w_conv_out[layer])
        lam = (jnp.exp(jnp.sum(lambda_q1[layer].astype(jnp.float32) * lambda_k1[layer].astype(jnp.float32)))
               - jnp.exp(jnp.sum(lambda_q2[layer].astype(jnp.float32) * lambda_k2[layer].astype(jnp.float32)))
               + lambda_init)
        y_c = diff_attention(q, k, v, rel_bias, lam, attn_subln_g[layer], lambda_init) @ w_attn_out[layer]

        m = gates[:, :, 0] * y_a + gates[:, :, 1] * y_b + gates[:, :, 2] * y_c
        x = x + rms_norm(m @ w_out[layer], post_mix_g[layer])
        h = rms_norm(x, pre_ffn_g[layer])
        gu = h @ w_ffn_in[layer]
        f = (jax.nn.silu(gu[..., :D_FF]) * gu[..., D_FF:]) @ w_ffn_out[layer]
        x = x + rms_norm(f, post_ffn_g[layer])
    return x
```

```python
import functools
import math

import jax
import jax.numpy as jnp
from jax import lax
from jax.experimental import pallas as pl
from jax.experimental.pallas import tpu as pltpu

F32 = jnp.float32
BF16 = jnp.bfloat16

D_MODEL = 1024
SSM_GROUP = 16
D_SSM = D_MODEL // 2
SSM_GROUPS = D_SSM // SSM_GROUP
SSM_STATE = 64
D_CONV = D_MODEL // 2
CONV_WIDTH = 31
ATTN_HEADS = 8
ATTN_HEAD_DIM = 64
ATTN_MAPS = 2 * ATTN_HEADS
D_QK = ATTN_MAPS * ATTN_HEAD_DIM
D_V = ATTN_HEADS * 2 * ATTN_HEAD_DIM
REL_BUCKETS = 32
REL_MAX_DIST = 128
D_FF = 2816
N_BRANCH = 3
OFF_SSM = 0
OFF_CONV = OFF_SSM + D_SSM
OFF_Q = OFF_CONV + 2 * D_CONV
OFF_K = OFF_Q + D_QK
OFF_V = OFF_K + D_QK
OFF_GATE = OFF_V + D_V
D_IN = OFF_GATE + N_BRANCH * D_MODEL
RMS_EPS = 1e-6
SUBLN_EPS = 1e-5
LN_EPS = 1e-5

V7X_VMEM_LIMIT_BYTES = 56 * 1024 * 1024
SUBLANES = 8
HEAD_LANES = 2 * ATTN_HEAD_DIM

SSM_CHUNK = 32
CONV_HALO = 32
ATTN_BLOCK = 256
INPROJ_ROWS = 256
MERGE_ROWS = 256
FFN_ROWS = 512
FFN_CHUNK = 256


def _params(sem):
    return pltpu.CompilerParams(dimension_semantics=sem, vmem_limit_bytes=V7X_VMEM_LIMIT_BYTES)


def _resident(shape):
    nd = len(shape)
    return pl.BlockSpec(shape, lambda *_: (0,) * nd, pipeline_mode=pl.Buffered(1))


def _rms(x, g):
    return x * lax.rsqrt(jnp.mean(x * x, axis=-1, keepdims=True) + RMS_EPS) * g


def _inproj_kernel(x_ref, g_ref, w_ref, u_ref, c_ref, q_ref, k_ref, v_ref, gl_ref):
    hb = _rms(x_ref[...], g_ref[...]).astype(BF16)

    def proj(lo, hi):
        return jnp.dot(hb, w_ref[:, lo:hi], preferred_element_type=F32)

    u_ref[...] = proj(OFF_SSM, OFF_CONV)
    c_ref[...] = proj(OFF_CONV, OFF_Q)
    q_ref[...] = (proj(OFF_Q, OFF_K) * (ATTN_HEAD_DIM ** -0.5)).astype(BF16)
    k_ref[...] = proj(OFF_K, OFF_V).astype(BF16)
    v_ref[...] = proj(OFF_V, OFF_GATE).astype(BF16)
    for b in range(N_BRANCH):
        lo = OFF_GATE + b * D_MODEL
        gl_ref[:, b * D_MODEL:(b + 1) * D_MODEL] = proj(lo, lo + D_MODEL)


def _inproj(x2, g, w):
    n = x2.shape[0]
    tm = INPROJ_ROWS
    row = lambda width: pl.BlockSpec((tm, width), lambda i: (i, 0))
    return pl.pallas_call(
        _inproj_kernel,
        grid=(n // tm,),
        in_specs=[row(D_MODEL), _resident((1, D_MODEL)), _resident((D_MODEL, D_IN))],
        out_specs=[row(D_SSM), row(2 * D_CONV), row(D_QK), row(D_QK), row(D_V), row(N_BRANCH * D_MODEL)],
        out_shape=[
            jax.ShapeDtypeStruct((n, D_SSM), F32),
            jax.ShapeDtypeStruct((n, 2 * D_CONV), F32),
            jax.ShapeDtypeStruct((n, D_QK), BF16),
            jax.ShapeDtypeStruct((n, D_QK), BF16),
            jax.ShapeDtypeStruct((n, D_V), BF16),
            jax.ShapeDtypeStruct((n, N_BRANCH * D_MODEL), F32),
        ],
        compiler_params=_params(("parallel",)),
        name="inproj",
    )(x2, g, w)


def _ssm_matrices(a_re, a_im, log_dt, b_re, b_im, c_re, c_im):
    hp = lax.Precision.HIGHEST
    t_len = SSM_CHUNK
    dt = jnp.exp(log_dt)[:, None]
    zr, zi = a_re * dt, a_im * dt
    er = jnp.exp(zr)
    lbr, lbi = er * jnp.cos(zi), er * jnp.sin(zi)
    den = a_re * a_re + a_im * a_im
    nr, ni = lbr - 1.0, lbi
    cr, ci = (nr * a_re + ni * a_im) / den, (ni * a_re - nr * a_im) / den
    bbr = cr[..., None] * b_re - ci[..., None] * b_im
    bbi = cr[..., None] * b_im + ci[..., None] * b_re
    tau = jnp.arange(t_len + 1, dtype=F32)[:, None, None]
    pe = jnp.exp(tau * zr)
    pwr, pwi = pe * jnp.cos(tau * zi), pe * jnp.sin(tau * zi)
    e_r = pwr[:t_len, :, :, None] * bbr - pwi[:t_len, :, :, None] * bbi
    e_i = pwr[:t_len, :, :, None] * bbi + pwi[:t_len, :, :, None] * bbr
    kern = (jnp.einsum("gcp,tgpd->tgcd", c_re, e_r, precision=hp)
            - jnp.einsum("gcp,tgpd->tgcd", c_im, e_i, precision=hp))
    lag = jnp.arange(t_len)[None, :] - jnp.arange(t_len)[:, None]
    toe = jnp.where((lag >= 0)[:, :, None, None, None], kern[jnp.clip(lag, 0, t_len - 1)], 0.0)
    gc = SSM_GROUP * t_len
    w_intra = toe.transpose(2, 0, 4, 1, 3).reshape(SSM_GROUPS, gc, gc)
    w_sr = e_r[::-1].transpose(1, 0, 3, 2).reshape(SSM_GROUPS, gc, SSM_STATE)
    w_si = e_i[::-1].transpose(1, 0, 3, 2).reshape(SSM_GROUPS, gc, SSM_STATE)
    p1r, p1i = pwr[1:, :, None, :], pwi[1:, :, None, :]
    w_or = (c_re * p1r - c_im * p1i).transpose(1, 3, 0, 2).reshape(SSM_GROUPS, SSM_STATE, gc)
    w_oi = (-(c_re * p1i + c_im * p1r)).transpose(1, 3, 0, 2).reshape(SSM_GROUPS, SSM_STATE, gc)
    a_r = pwr[t_len][:, None, :]
    a_i = pwi[t_len][:, None, :]
    return (w_intra.astype(BF16), w_sr.astype(BF16), w_si.astype(BF16),
            w_or.astype(BF16), w_oi.astype(BF16), a_r, a_i)


def _ssm_kernel(u_ref, wi_ref, wsr_ref, wsi_ref, wor_ref, woi_ref, ar_ref, ai_ref, d_ref,
                o_ref, sr_ref, si_ref, xr_ref, xi_ref, *, n_batch, n_chunk):
    u = u_ref[0]
    ub = u.astype(BF16)
    sr_ref[...] = jnp.dot(ub, wsr_ref[0], preferred_element_type=F32)
    si_ref[...] = jnp.dot(ub, wsi_ref[0], preferred_element_type=F32)
    a_r, a_i = ar_ref[0], ai_ref[0]

    def step(c, carry):
        nxt = []
        for b in range(n_batch):
            x_r, x_i = carry[2 * b], carry[2 * b + 1]
            row = pl.ds(b * n_chunk + c, 1)
            xr_ref[row, :] = x_r
            xi_ref[row, :] = x_i
            nxt.append(a_r * x_r - a_i * x_i + sr_ref[row, :])
            nxt.append(a_r * x_i + a_i * x_r + si_ref[row, :])
        return tuple(nxt)

    zero = jnp.zeros((1, SSM_STATE), F32)
    lax.fori_loop(0, n_chunk, step, (zero,) * (2 * n_batch))
    y = (jnp.dot(ub, wi_ref[0], preferred_element_type=F32)
         + jnp.dot(xr_ref[...].astype(BF16), wor_ref[0], preferred_element_type=F32)
         + jnp.dot(xi_ref[...].astype(BF16), woi_ref[0], preferred_element_type=F32)
         + d_ref[0] * u)
    o_ref[0] = jax.nn.gelu(y, approximate=True).astype(BF16)


def _ssm(u2, mats, d_skip, n_batch, length):
    t_len = SSM_CHUNK
    n_chunk = length // t_len
    rows = n_batch * n_chunk
    gc = SSM_GROUP * t_len
    ug = (u2.reshape(rows, t_len, SSM_GROUPS, SSM_GROUP).transpose(2, 0, 1, 3)
          .reshape(SSM_GROUPS, rows, gc))
    d_t = jnp.tile(d_skip[:, None, :], (1, t_len, 1)).reshape(SSM_GROUPS, 1, gc)
    w_intra, w_sr, w_si, w_or, w_oi, a_r, a_i = mats
    grp = lambda *shape: pl.BlockSpec((1,) + shape, lambda g: (g, 0, 0))
    yg = pl.pallas_call(
        functools.partial(_ssm_kernel, n_batch=n_batch, n_chunk=n_chunk),
        grid=(SSM_GROUPS,),
        in_specs=[grp(rows, gc), grp(gc, gc), grp(gc, SSM_STATE), grp(gc, SSM_STATE),
                  grp(SSM_STATE, gc), grp(SSM_STATE, gc), grp(1, SSM_STATE), grp(1, SSM_STATE),
                  grp(1, gc)],
        out_specs=grp(rows, gc),
        out_shape=jax.ShapeDtypeStruct((SSM_GROUPS, rows, gc), BF16),
        scratch_shapes=[pltpu.VMEM((rows, SSM_STATE), F32)] * 4,
        compiler_params=_params(("parallel",)),
        name="ssm",
    )(ug, w_intra, w_sr, w_si, w_or, w_oi, a_r, a_i, d_t)
    return (yg.reshape(SSM_GROUPS, rows, t_len, SSM_GROUP).transpose(1, 2, 0, 3)
            .reshape(n_batch * length, D_SSM))


def _rel_bucket(rel):
    n = jnp.maximum(rel, 0)
    max_exact = REL_BUCKETS // 2
    nf = jnp.maximum(n, max_exact).astype(F32)
    large = max_exact + (jnp.log(nf / max_exact) / math.log(REL_MAX_DIST / max_exact)
                         * (REL_BUCKETS - max_exact)).astype(jnp.int32)
    large = jnp.minimum(large, REL_BUCKETS - 1)
    return jnp.where(n < max_exact, n, large)


def _attn_bias(rel_bias):
    blk = ATTN_BLOCK
    assert blk + 1 >= REL_MAX_DIST
    rb = rel_bias.astype(F32)
    rel_diag = jnp.arange(blk)[None, :] - jnp.arange(blk)[:, None]
    b_diag = jnp.where((rel_diag >= 0)[:, :, None], rb[_rel_bucket(rel_diag)], -jnp.inf)
    b_prev = rb[_rel_bucket(rel_diag + blk)]
    near = jnp.stack([b_prev, b_diag])
    near = near.reshape(2, blk, blk, ATTN_HEADS, 2).transpose(3, 0, 1, 4, 2)
    near = near.reshape(ATTN_HEADS, 2, blk, 2 * blk)
    far = rb[_rel_bucket(jnp.full((), blk + 1, jnp.int32))]
    far = jnp.broadcast_to(far.reshape(ATTN_HEADS, 2, 1), (ATTN_HEADS, 2, blk))
    return near, far.reshape(ATTN_HEADS, 1, 2 * blk)


def _attn_kernel(sc_ref, q_ref, k_ref, vt_ref, near_ref, far_ref, g_ref, o_ref,
                 qs_ref, m_ref, l_ref, acc_ref):
    blk = ATTN_BLOCK
    i = pl.program_id(2)
    q = q_ref[...]
    lane = lax.broadcasted_iota(jnp.int32, q.shape, 1)
    zero = jnp.zeros_like(q)
    qs_ref[0:blk, :] = jnp.where(lane < ATTN_HEAD_DIM, q, zero)
    qs_ref[blk:, :] = jnp.where(lane >= ATTN_HEAD_DIM, q, zero)
    m_ref[...] = jnp.full(m_ref.shape, -jnp.inf, F32)
    l_ref[...] = jnp.zeros(l_ref.shape, F32)
    acc_ref[...] = jnp.zeros(acc_ref.shape, F32)

    def block(j, bias, far):
        kb = k_ref[pl.ds(pl.multiple_of(j * blk, blk), blk), :]
        s = lax.dot_general(kb, qs_ref[...], (((1,), (1,)), ((), ())),
                            preferred_element_type=F32)
        if bias is not None:
            s = s + bias
        cm = jnp.max(s, axis=0, keepdims=True)
        if far is not None:
            cm = cm + far
        m_old = m_ref[...]
        m_new = jnp.maximum(m_old, cm)
        shift = m_new if far is None else m_new - far
        p = jnp.exp(s - shift)
        alpha = jnp.exp(m_old - m_new)
        l_ref[...] = alpha * l_ref[...] + jnp.sum(p, axis=0, keepdims=True)
        acc_ref[...] = alpha * acc_ref[...] + jnp.dot(vt_ref[j], p.astype(BF16),
                                                      preferred_element_type=F32)
        m_ref[...] = m_new

    def far_step(j, carry):
        block(j, None, far_ref[0])
        return carry

    lax.fori_loop(0, i - 1, far_step, 0)

    @pl.when(i >= 1)
    def _():
        block(i - 1, near_ref[0, 0], None)

    block(i, near_ref[0, 1], None)

    lam, out_scale = sc_ref[0], sc_ref[1]
    a = acc_ref[...] * (1.0 / l_ref[...])
    o = a[:, :blk] - lam * a[:, blk:]
    o = o * lax.rsqrt(jnp.mean(o * o, axis=0, keepdims=True) + SUBLN_EPS) * g_ref[...] * out_scale
    o_ref[...] = o.T.astype(BF16)


def _attention(q2, k2, v2, near, far, scalars, subln_g, n_batch, length):
    blk = ATTN_BLOCK
    nq = length // blk
    n = n_batch * length
    vt = (v2.reshape(n_batch, nq, blk, ATTN_HEADS, HEAD_LANES).transpose(0, 3, 1, 4, 2))
    return pl.pallas_call(
        _attn_kernel,
        grid=(n_batch, ATTN_HEADS, nq),
        in_specs=[
            pl.BlockSpec(memory_space=pltpu.SMEM),
            pl.BlockSpec((blk, HEAD_LANES), lambda b, h, i: (b * nq + i, h)),
            pl.BlockSpec((length, HEAD_LANES), lambda b, h, i: (b, h)),
            pl.BlockSpec((None, None, nq, HEAD_LANES, blk), lambda b, h, i: (b, h, 0, 0, 0)),
            pl.BlockSpec((1, 2, blk, 2 * blk), lambda b, h, i: (h, 0, 0, 0)),
            pl.BlockSpec((1, 1, 2 * blk), lambda b, h, i: (h, 0, 0)),
            pl.BlockSpec((HEAD_LANES, 1), lambda b, h, i: (0, 0)),
        ],
        out_specs=pl.BlockSpec((blk, HEAD_LANES), lambda b, h, i: (b * nq + i, h)),
        out_shape=jax.ShapeDtypeStruct((n, D_V), BF16),
        scratch_shapes=[
            pltpu.VMEM((2 * blk, HEAD_LANES), BF16),
            pltpu.VMEM((1, 2 * blk), F32),
            pltpu.VMEM((1, 2 * blk), F32),
            pltpu.VMEM((HEAD_LANES, 2 * blk), F32),
        ],
        compiler_params=_params(("parallel", "parallel", "arbitrary")),
        name="diff_attention",
    )(scalars, q2, k2, vt, near, far, subln_g.reshape(HEAD_LANES, 1))


def _merge_kernel(x_ref, ys_ref, cc_ref, cp_ref, ao_ref, gl_ref, wglu_ref, wso_ref, dw_ref, dwb_ref,
                  lng_ref, lnb_ref, wco_ref, wao_ref, wout_ref, pg_ref, o_ref, gbuf_ref,
                  *, tiles_per_seq):
    tm = MERGE_ROWS
    i = pl.program_id(0)
    z = jnp.dot(ys_ref[...], wglu_ref[...], preferred_element_type=F32)
    glu = (z[:, :D_SSM] * jax.nn.sigmoid(z[:, D_SSM:])).astype(BF16)
    y_a = jnp.dot(glu, wso_ref[...], preferred_element_type=F32)
    m = jax.nn.sigmoid(gl_ref[:, 0:D_MODEL]) * y_a
    cc = cc_ref[...]
    cp = cp_ref[...]
    g_prev = cp[:, :D_CONV] * jax.nn.sigmoid(cp[:, D_CONV:])
    g_prev = jnp.where(i % tiles_per_seq == 0, jnp.zeros_like(g_prev), g_prev)
    gbuf_ref[0:CONV_HALO, :] = g_prev
    gbuf_ref[CONV_HALO:, :] = cc[:, :D_CONV] * jax.nn.sigmoid(cc[:, D_CONV:])
    y = jnp.zeros((tm, D_CONV), F32) + dwb_ref[...]
    base = CONV_HALO - (CONV_WIDTH - 1)
    for tap in range(CONV_WIDTH):
        y = y + gbuf_ref[base + tap:base + tap + tm, :] * dw_ref[tap:tap + 1, :]
    mu = jnp.mean(y, axis=-1, keepdims=True)
    yc = y - mu
    var = jnp.mean(yc * yc, axis=-1, keepdims=True)
    yn = yc * lax.rsqrt(var + LN_EPS) * lng_ref[...] + lnb_ref[...]
    y_b = jnp.dot(jax.nn.silu(yn).astype(BF16), wco_ref[...], preferred_element_type=F32)
    m = m + jax.nn.sigmoid(gl_ref[:, D_MODEL:2 * D_MODEL]) * y_b
    y_c = jnp.dot(ao_ref[...], wao_ref[...], preferred_element_type=F32)
    m = m + jax.nn.sigmoid(gl_ref[:, 2 * D_MODEL:3 * D_MODEL]) * y_c
    r = jnp.dot(m.astype(BF16), wout_ref[...], preferred_element_type=F32)
    o_ref[...] = x_ref[...] + _rms(r, pg_ref[...])


def _merge(x2, ys, c_in, ao, gl, w_glu, w_so, dw, dw_b, ln_g, ln_b, w_co, w_ao, w_out, post_g, length):
    n = x2.shape[0]
    tm = MERGE_ROWS
    halo_blocks = tm // CONV_HALO
    row = lambda width: pl.BlockSpec((tm, width), lambda i: (i, 0))
    dw_pad = jnp.zeros((CONV_HALO, D_CONV), F32).at[:CONV_WIDTH].set(dw)
    return pl.pallas_call(
        functools.partial(_merge_kernel, tiles_per_seq=length // tm),
        grid=(n // tm,),
        in_specs=[
            row(D_MODEL), row(D_SSM), row(2 * D_CONV),
            pl.BlockSpec((CONV_HALO, 2 * D_CONV), lambda i: (jnp.maximum(i * halo_blocks - 1, 0), 0)),
            row(D_V), row(N_BRANCH * D_MODEL),
            _resident((D_SSM, 2 * D_SSM)), _resident((D_SSM, D_MODEL)),
            _resident((CONV_HALO, D_CONV)), _resident((1, D_CONV)), _resident((1, D_CONV)),
            _resident((1, D_CONV)), _resident((D_CONV, D_MODEL)), _resident((D_V, D_MODEL)),
            _resident((D_MODEL, D_MODEL)), _resident((1, D_MODEL)),
        ],
        out_specs=row(D_MODEL),
        out_shape=jax.ShapeDtypeStruct((n, D_MODEL), F32),
        scratch_shapes=[pltpu.VMEM((CONV_HALO + tm, D_CONV), F32)],
        compiler_params=_params(("parallel",)),
        name="merge",
    )(x2, ys, c_in, c_in, ao, gl, w_glu, w_so, dw_pad, dw_b, ln_g, ln_b, w_co, w_ao, w_out, post_g)


def _ffn_kernel(x_ref, g1_ref, win_ref, wout_ref, g2_ref, o_ref):
    x = x_ref[...]
    hb = _rms(x, g1_ref[...]).astype(BF16)
    f = jnp.zeros(x.shape, F32)
    for c in range(D_FF // FFN_CHUNK):
        lo = c * FFN_CHUNK
        gate = jnp.dot(hb, win_ref[:, lo:lo + FFN_CHUNK], preferred_element_type=F32)
        up = jnp.dot(hb, win_ref[:, D_FF + lo:D_FF + lo + FFN_CHUNK], preferred_element_type=F32)
        act = (jax.nn.silu(gate) * up).astype(BF16)
        f = f + jnp.dot(act, wout_ref[lo:lo + FFN_CHUNK, :], preferred_element_type=F32)
    o_ref[...] = x + _rms(f, g2_ref[...])


def _ffn(x2, pre_g, w_in, w_out, post_g):
    n = x2.shape[0]
    tm = FFN_ROWS
    row = pl.BlockSpec((tm, D_MODEL), lambda i: (i, 0))
    return pl.pallas_call(
        _ffn_kernel,
        grid=(n // tm,),
        in_specs=[row, _resident((1, D_MODEL)), _resident((D_MODEL, 2 * D_FF)),
                  _resident((D_FF, D_MODEL)), _resident((1, D_MODEL))],
        out_specs=row,
        out_shape=jax.ShapeDtypeStruct((n, D_MODEL), F32),
        compiler_params=_params(("parallel",)),
        name="ffn",
    )(x2, pre_g, w_in, w_out, post_g)


def kernel(x, rel_bias, pre_mix_g, w_in, ssm_a_re, ssm_a_im, ssm_log_dt, ssm_b_re, ssm_b_im, ssm_c_re, ssm_c_im, ssm_d, w_ssm_glu, w_ssm_out, conv_dw, conv_dw_b, conv_ln_g, conv_ln_b, w_conv_out, lambda_q1, lambda_k1, lambda_q2, lambda_k2, attn_subln_g, w_attn_out, w_out, post_mix_g, pre_ffn_g, w_ffn_in, w_ffn_out, post_ffn_g):
    n_batch, length, _ = x.shape
    depth = w_in.shape[0]
    assert length % max(ATTN_BLOCK, MERGE_ROWS, FFN_ROWS, INPROJ_ROWS, SSM_CHUNK) == 0
    x2 = x.astype(F32).reshape(n_batch * length, D_MODEL)
    near, far = _attn_bias(rel_bias)
    vec = lambda a: a.astype(F32).reshape(1, -1)
    for layer in range(depth):
        lambda_init = 0.8 - 0.6 * math.exp(-0.3 * layer)
        lam = (jnp.exp(jnp.sum(lambda_q1[layer].astype(F32) * lambda_k1[layer].astype(F32)))
               - jnp.exp(jnp.sum(lambda_q2[layer].astype(F32) * lambda_k2[layer].astype(F32)))
               + lambda_init)
        scalars = jnp.stack([lam, jnp.asarray(1.0 - lambda_init, F32)]).astype(F32)
        u, c_in, q, k, v, gl = _inproj(x2, vec(pre_mix_g[layer]), w_in[layer].astype(BF16))
        mats = _ssm_matrices(ssm_a_re[layer].astype(F32), ssm_a_im[layer].astype(F32),
                             ssm_log_dt[layer].astype(F32), ssm_b_re[layer].astype(F32),
                             ssm_b_im[layer].astype(F32), ssm_c_re[layer].astype(F32),
                             ssm_c_im[layer].astype(F32))
        ys = _ssm(u, mats, ssm_d[layer].astype(F32), n_batch, length)
        ao = _attention(q, k, v, near, far, scalars, attn_subln_g[layer].astype(F32), n_batch, length)
        x2 = _merge(x2, ys, c_in, ao, gl, w_ssm_glu[layer].astype(BF16), w_ssm_out[layer].astype(BF16),
                    conv_dw[layer].astype(F32), vec(conv_dw_b[layer]), vec(conv_ln_g[layer]),
                    vec(conv_ln_b[layer]), w_conv_out[layer].astype(BF16), w_attn_out[layer].astype(BF16),
                    w_out[layer].astype(BF16), vec(post_mix_g[layer]), length)
        x2 = _ffn(x2, vec(pre_ffn_g[layer]), w_ffn_in[layer].astype(BF16), w_ffn_out[layer].astype(BF16),
                  vec(post_ffn_g[layer]))
    return x2.reshape(n_batch, length, D_MODEL).astype(x.dtype)
```

```python
import functools
import math

import jax
import jax.numpy as jnp
from jax import lax
from jax.experimental import pallas as pl
from jax.experimental.pallas import tpu as pltpu

F32 = jnp.float32
BF16 = jnp.bfloat16

D_MODEL = 1024
SSM_GROUP = 16
D_SSM = D_MODEL // 2
SSM_GROUPS = D_SSM // SSM_GROUP
SSM_STATE = 64
D_CONV = D_MODEL // 2
CONV_WIDTH = 31
ATTN_HEADS = 8
ATTN_HEAD_DIM = 64
ATTN_MAPS = 2 * ATTN_HEADS
D_QK = ATTN_MAPS * ATTN_HEAD_DIM
D_V = ATTN_HEADS * 2 * ATTN_HEAD_DIM
REL_BUCKETS = 32
REL_MAX_DIST = 128
D_FF = 2816
N_BRANCH = 3
OFF_SSM = 0
OFF_CONV = OFF_SSM + D_SSM
OFF_Q = OFF_CONV + 2 * D_CONV
OFF_K = OFF_Q + D_QK
OFF_V = OFF_K + D_QK
OFF_GATE = OFF_V + D_V
D_IN = OFF_GATE + N_BRANCH * D_MODEL
RMS_EPS = 1e-6
SUBLN_EPS = 1e-5
LN_EPS = 1e-5
LOG2_E = math.log2(math.e)

V7X_VMEM_LIMIT_BYTES = 56 * 1024 * 1024
SUBLANES = 8
HEAD_LANES = 2 * ATTN_HEAD_DIM

SSM_CHUNK = 32
CONV_HALO = 32
ATTN_BLOCK = 512
INPROJ_ROWS = 256
MERGE_ROWS = 256
FFN_ROWS = 512
FFN_CHUNK = 256


def _params(sem):
    return pltpu.CompilerParams(dimension_semantics=sem, vmem_limit_bytes=V7X_VMEM_LIMIT_BYTES)


def _resident(shape):
    nd = len(shape)
    return pl.BlockSpec(shape, lambda *_: (0,) * nd, pipeline_mode=pl.Buffered(1))


def _rms(x, g):
    return x * lax.rsqrt(jnp.mean(x * x, axis=-1, keepdims=True) + RMS_EPS) * g


def _inproj_kernel(x_ref, g_ref, w_ref, u_ref, c_ref, q_ref, k_ref, v_ref, gl_ref):
    hb = _rms(x_ref[...], g_ref[...]).astype(BF16)

    def proj(lo, hi):
        return jnp.dot(hb, w_ref[:, lo:hi], preferred_element_type=F32)

    u_ref[...] = proj(OFF_SSM, OFF_CONV)
    c_ref[...] = proj(OFF_CONV, OFF_Q)
    q_ref[...] = (proj(OFF_Q, OFF_K) * (LOG2_E * ATTN_HEAD_DIM ** -0.5)).astype(BF16)
    k_ref[...] = proj(OFF_K, OFF_V).astype(BF16)
    v_ref[...] = proj(OFF_V, OFF_GATE).astype(BF16)
    for b in range(N_BRANCH):
        lo = OFF_GATE + b * D_MODEL
        gl_ref[:, b * D_MODEL:(b + 1) * D_MODEL] = proj(lo, lo + D_MODEL)


def _inproj(x2, g, w):
    n = x2.shape[0]
    tm = INPROJ_ROWS
    row = lambda width: pl.BlockSpec((tm, width), lambda i: (i, 0))
    return pl.pallas_call(
        _inproj_kernel,
        grid=(n // tm,),
        in_specs=[row(D_MODEL), _resident((1, D_MODEL)), _resident((D_MODEL, D_IN))],
        out_specs=[row(D_SSM), row(2 * D_CONV), row(D_QK), row(D_QK), row(D_V), row(N_BRANCH * D_MODEL)],
        out_shape=[
            jax.ShapeDtypeStruct((n, D_SSM), F32),
            jax.ShapeDtypeStruct((n, 2 * D_CONV), F32),
            jax.ShapeDtypeStruct((n, D_QK), BF16),
            jax.ShapeDtypeStruct((n, D_QK), BF16),
            jax.ShapeDtypeStruct((n, D_V), BF16),
            jax.ShapeDtypeStruct((n, N_BRANCH * D_MODEL), F32),
        ],
        compiler_params=_params(("parallel",)),
        name="inproj",
    )(x2, g, w)


def _ssm_matrices(a_re, a_im, log_dt, b_re, b_im, c_re, c_im):
    hp = lax.Precision.HIGHEST
    t_len = SSM_CHUNK
    dt = jnp.exp(log_dt)[:, None]
    zr, zi = a_re * dt, a_im * dt
    er = jnp.exp(zr)
    lbr, lbi = er * jnp.cos(zi), er * jnp.sin(zi)
    den = a_re * a_re + a_im * a_im
    nr, ni = lbr - 1.0, lbi
    cr, ci = (nr * a_re + ni * a_im) / den, (ni * a_re - nr * a_im) / den
    bbr = cr[..., None] * b_re - ci[..., None] * b_im
    bbi = cr[..., None] * b_im + ci[..., None] * b_re
    tau = jnp.arange(t_len + 1, dtype=F32)[:, None, None]
    pe = jnp.exp(tau * zr)
    pwr, pwi = pe * jnp.cos(tau * zi), pe * jnp.sin(tau * zi)
    e_r = pwr[:t_len, :, :, None] * bbr - pwi[:t_len, :, :, None] * bbi
    e_i = pwr[:t_len, :, :, None] * bbi + pwi[:t_len, :, :, None] * bbr
    kern = (jnp.einsum("gcp,tgpd->tgcd", c_re, e_r, precision=hp)
            - jnp.einsum("gcp,tgpd->tgcd", c_im, e_i, precision=hp))
    lag = jnp.arange(t_len)[None, :] - jnp.arange(t_len)[:, None]
    toe = jnp.where((lag >= 0)[:, :, None, None, None], kern[jnp.clip(lag, 0, t_len - 1)], 0.0)
    gc = SSM_GROUP * t_len
    w_intra = toe.transpose(2, 0, 4, 1, 3).reshape(SSM_GROUPS, gc, gc)
    w_sr = e_r[::-1].transpose(1, 0, 3, 2).reshape(SSM_GROUPS, gc, SSM_STATE)
    w_si = e_i[::-1].transpose(1, 0, 3, 2).reshape(SSM_GROUPS, gc, SSM_STATE)
    p1r, p1i = pwr[1:, :, None, :], pwi[1:, :, None, :]
    w_or = (c_re * p1r - c_im * p1i).transpose(1, 3, 0, 2).reshape(SSM_GROUPS, SSM_STATE, gc)
    w_oi = (-(c_re * p1i + c_im * p1r)).transpose(1, 3, 0, 2).reshape(SSM_GROUPS, SSM_STATE, gc)
    a_r = pwr[t_len][:, None, :]
    a_i = pwi[t_len][:, None, :]
    return (w_intra.astype(BF16), w_sr.astype(BF16), w_si.astype(BF16),
            w_or.astype(BF16), w_oi.astype(BF16), a_r, a_i)


def _ssm_kernel(u_ref, wi_ref, wsr_ref, wsi_ref, wor_ref, woi_ref, ar_ref, ai_ref, d_ref,
                o_ref, sr_ref, si_ref, xr_ref, xi_ref, *, n_batch, n_chunk):
    u = u_ref[0]
    ub = u.astype(BF16)
    sr_ref[...] = jnp.dot(ub, wsr_ref[0], preferred_element_type=F32)
    si_ref[...] = jnp.dot(ub, wsi_ref[0], preferred_element_type=F32)
    a_r, a_i = ar_ref[0], ai_ref[0]

    def step(c, carry):
        nxt = []
        for b in range(n_batch):
            x_r, x_i = carry[2 * b], carry[2 * b + 1]
            row = pl.ds(b * n_chunk + c, 1)
            xr_ref[row, :] = x_r
            xi_ref[row, :] = x_i
            nxt.append(a_r * x_r - a_i * x_i + sr_ref[row, :])
            nxt.append(a_r * x_i + a_i * x_r + si_ref[row, :])
        return tuple(nxt)

    zero = jnp.zeros((1, SSM_STATE), F32)
    lax.fori_loop(0, n_chunk, step, (zero,) * (2 * n_batch))
    y = (jnp.dot(ub, wi_ref[0], preferred_element_type=F32)
         + jnp.dot(xr_ref[...].astype(BF16), wor_ref[0], preferred_element_type=F32)
         + jnp.dot(xi_ref[...].astype(BF16), woi_ref[0], preferred_element_type=F32)
         + d_ref[0] * u)
    o_ref[0] = jax.nn.gelu(y, approximate=True).astype(BF16)


def _ssm(u2, mats, d_skip, n_batch, length):
    t_len = SSM_CHUNK
    n_chunk = length // t_len
    rows = n_batch * n_chunk
    gc = SSM_GROUP * t_len
    ug = (u2.reshape(rows, t_len, SSM_GROUPS, SSM_GROUP).transpose(2, 0, 1, 3)
          .reshape(SSM_GROUPS, rows, gc))
    d_t = jnp.tile(d_skip[:, None, :], (1, t_len, 1)).reshape(SSM_GROUPS, 1, gc)
    w_intra, w_sr, w_si, w_or, w_oi, a_r, a_i = mats
    grp = lambda *shape: pl.BlockSpec((1,) + shape, lambda g: (g, 0, 0))
    yg = pl.pallas_call(
        functools.partial(_ssm_kernel, n_batch=n_batch, n_chunk=n_chunk),
        grid=(SSM_GROUPS,),
        in_specs=[grp(rows, gc), grp(gc, gc), grp(gc, SSM_STATE), grp(gc, SSM_STATE),
                  grp(SSM_STATE, gc), grp(SSM_STATE, gc), grp(1, SSM_STATE), grp(1, SSM_STATE),
                  grp(1, gc)],
        out_specs=grp(rows, gc),
        out_shape=jax.ShapeDtypeStruct((SSM_GROUPS, rows, gc), BF16),
        scratch_shapes=[pltpu.VMEM((rows, SSM_STATE), F32)] * 4,
        compiler_params=_params(("parallel",)),
        name="ssm",
    )(ug, w_intra, w_sr, w_si, w_or, w_oi, a_r, a_i, d_t)
    return (yg.reshape(SSM_GROUPS, rows, t_len, SSM_GROUP).transpose(1, 2, 0, 3)
            .reshape(n_batch * length, D_SSM))


def _rel_bucket(rel):
    n = jnp.maximum(rel, 0)
    max_exact = REL_BUCKETS // 2
    nf = jnp.maximum(n, max_exact).astype(F32)
    large = max_exact + (jnp.log(nf / max_exact) / math.log(REL_MAX_DIST / max_exact)
                         * (REL_BUCKETS - max_exact)).astype(jnp.int32)
    large = jnp.minimum(large, REL_BUCKETS - 1)
    return jnp.where(n < max_exact, n, large)


def _attn_bias(rel_bias):
    blk = ATTN_BLOCK
    assert blk + 1 >= REL_MAX_DIST
    rb = rel_bias.astype(F32) * LOG2_E
    rel = jnp.arange(-blk, 2 * blk + 1)
    table = jnp.where((rel >= 0)[None, :], rb[_rel_bucket(rel)].T, -jnp.inf)

    def toeplitz(v):
        rows = jnp.broadcast_to(v[:, None, :], (ATTN_MAPS, blk, 2 * blk)).reshape(ATTN_MAPS, -1)
        skew = rows[:, :blk * (2 * blk - 1)].reshape(ATTN_MAPS, blk, 2 * blk - 1)
        return skew[:, :, blk - 1:]

    b_diag = toeplitz(table[:, 1:2 * blk + 1])
    b_prev = toeplitz(table[:, blk + 1:])
    near = jnp.stack([b_prev, b_diag], axis=1)
    near = near.reshape(ATTN_HEADS, 2, 2, blk, blk).transpose(0, 2, 3, 1, 4)
    near = near.reshape(ATTN_HEADS, 2, blk, 2 * blk)
    far = rb[_rel_bucket(jnp.full((), blk + 1, jnp.int32))]
    far = jnp.broadcast_to(far.reshape(ATTN_HEADS, 2, 1), (ATTN_HEADS, 2, blk))
    return near, far.reshape(ATTN_HEADS, 1, 2 * blk)


def _attn_kernel(sc_ref, q_ref, k_ref, vt_ref, near_ref, far_ref, g_ref, o_ref,
                 qs_ref, sa_ref, sb_ref, m_ref, l_ref, acc_ref):
    blk = ATTN_BLOCK
    i = pl.program_id(2)
    q = q_ref[...]
    lane = lax.broadcasted_iota(jnp.int32, q.shape, 1)
    zero = jnp.zeros_like(q)
    qs_ref[0:blk, :] = jnp.where(lane < ATTN_HEAD_DIM, q, zero)
    qs_ref[blk:, :] = jnp.where(lane >= ATTN_HEAD_DIM, q, zero)
    m_ref[...] = jnp.full(m_ref.shape, -jnp.inf, F32)
    l_ref[...] = jnp.zeros(l_ref.shape, F32)
    acc_ref[...] = jnp.zeros(acc_ref.shape, F32)

    def scores(j, dst_ref):
        j = jnp.maximum(j, 0)
        kb = k_ref[pl.ds(pl.multiple_of(j * blk, blk), blk), :]
        dst_ref[...] = lax.dot_general(kb, qs_ref[...], (((1,), (1,)), ((), ())),
                                       preferred_element_type=F32)

    def softmax_pv(j, src_ref, bias, far):
        s = src_ref[...]
        if bias is not None:
            s = s + bias
        cm = jnp.max(s, axis=0, keepdims=True)
        if far is not None:
            cm = cm + far
        m_old = m_ref[...]
        m_new = jnp.maximum(m_old, cm)
        shift = m_new if far is None else m_new - far
        p = jnp.exp2(s - shift)
        alpha = jnp.exp2(m_old - m_new)
        l_ref[...] = alpha * l_ref[...] + jnp.sum(p, axis=0, keepdims=True)
        acc_ref[...] = alpha * acc_ref[...] + jnp.dot(vt_ref[j], p.astype(BF16),
                                                      preferred_element_type=F32)
        m_ref[...] = m_new

    far = far_ref[0]
    scores(i, sa_ref)
    scores(i - 1, sb_ref)
    softmax_pv(i, sa_ref, near_ref[0, 1], None)

    @pl.when(i >= 1)
    def _():
        scores(i - 2, sa_ref)
        softmax_pv(i - 1, sb_ref, near_ref[0, 0], None)

    n_far = jnp.maximum(i - 1, 0)

    def far_pair(t, carry):
        j = i - 2 - 2 * t
        scores(j - 1, sb_ref)
        softmax_pv(j, sa_ref, None, far)
        scores(j - 2, sa_ref)
        softmax_pv(j - 1, sb_ref, None, far)
        return carry

    lax.fori_loop(0, n_far // 2, far_pair, 0)

    @pl.when(n_far % 2 == 1)
    def _():
        softmax_pv(0, sa_ref, None, far)

    lam, out_scale = sc_ref[0], sc_ref[1]
    a = acc_ref[...] * (1.0 / l_ref[...])
    o = a[:, :blk] - lam * a[:, blk:]
    o = o * lax.rsqrt(jnp.mean(o * o, axis=0, keepdims=True) + SUBLN_EPS) * g_ref[...] * out_scale
    o_ref[...] = o.T.astype(BF16)


def _attention(q2, k2, v2, near, far, scalars, subln_g, n_batch, length):
    blk = ATTN_BLOCK
    nq = length // blk
    n = n_batch * length
    vt = (v2.reshape(n_batch, nq, blk, ATTN_HEADS, HEAD_LANES).transpose(0, 3, 1, 4, 2))
    return pl.pallas_call(
        _attn_kernel,
        grid=(n_batch, ATTN_HEADS, nq),
        in_specs=[
            pl.BlockSpec(memory_space=pltpu.SMEM),
            pl.BlockSpec((blk, HEAD_LANES), lambda b, h, i: (b * nq + i, h)),
            pl.BlockSpec((length, HEAD_LANES), lambda b, h, i: (b, h)),
            pl.BlockSpec((None, None, nq, HEAD_LANES, blk), lambda b, h, i: (b, h, 0, 0, 0)),
            pl.BlockSpec((1, 2, blk, 2 * blk), lambda b, h, i: (h, 0, 0, 0)),
            pl.BlockSpec((1, 1, 2 * blk), lambda b, h, i: (h, 0, 0)),
            pl.BlockSpec((HEAD_LANES, 1), lambda b, h, i: (0, 0)),
        ],
        out_specs=pl.BlockSpec((blk, HEAD_LANES), lambda b, h, i: (b * nq + i, h)),
        out_shape=jax.ShapeDtypeStruct((n, D_V), BF16),
        scratch_shapes=[
            pltpu.VMEM((2 * blk, HEAD_LANES), BF16),
            pltpu.VMEM((blk, 2 * blk), F32),
            pltpu.VMEM((blk, 2 * blk), F32),
            pltpu.VMEM((1, 2 * blk), F32),
            pltpu.VMEM((1, 2 * blk), F32),
            pltpu.VMEM((HEAD_LANES, 2 * blk), F32),
        ],
        compiler_params=_params(("parallel", "parallel", "arbitrary")),
        name="diff_attention",
    )(scalars, q2, k2, vt, near, far, subln_g.reshape(HEAD_LANES, 1))


def _merge_kernel(x_ref, ys_ref, cc_ref, cp_ref, ao_ref, gl_ref, wglu_ref, wso_ref, dw_ref, dwb_ref,
                  lng_ref, lnb_ref, wco_ref, wao_ref, wout_ref, pg_ref, o_ref, gbuf_ref,
                  *, tiles_per_seq):
    tm = MERGE_ROWS
    i = pl.program_id(0)
    z = jnp.dot(ys_ref[...], wglu_ref[...], preferred_element_type=F32)
    glu = (z[:, :D_SSM] * jax.nn.sigmoid(z[:, D_SSM:])).astype(BF16)
    y_a = jnp.dot(glu, wso_ref[...], preferred_element_type=F32)
    m = jax.nn.sigmoid(gl_ref[:, 0:D_MODEL]) * y_a
    cc = cc_ref[...]
    cp = cp_ref[...]
    g_prev = cp[:, :D_CONV] * jax.nn.sigmoid(cp[:, D_CONV:])
    g_prev = jnp.where(i % tiles_per_seq == 0, jnp.zeros_like(g_prev), g_prev)
    gbuf_ref[0:CONV_HALO, :] = g_prev
    gbuf_ref[CONV_HALO:, :] = cc[:, :D_CONV] * jax.nn.sigmoid(cc[:, D_CONV:])
    y = jnp.zeros((tm, D_CONV), F32) + dwb_ref[...]
    base = CONV_HALO - (CONV_WIDTH - 1)
    for tap in range(CONV_WIDTH):
        y = y + gbuf_ref[base + tap:base + tap + tm, :] * dw_ref[tap:tap + 1, :]
    mu = jnp.mean(y, axis=-1, keepdims=True)
    yc = y - mu
    var = jnp.mean(yc * yc, axis=-1, keepdims=True)
    yn = yc * lax.rsqrt(var + LN_EPS) * lng_ref[...] + lnb_ref[...]
    y_b = jnp.dot(jax.nn.silu(yn).astype(BF16), wco_ref[...], preferred_element_type=F32)
    m = m + jax.nn.sigmoid(gl_ref[:, D_MODEL:2 * D_MODEL]) * y_b
    y_c = jnp.dot(ao_ref[...], wao_ref[...], preferred_element_type=F32)
    m = m + jax.nn.sigmoid(gl_ref[:, 2 * D_MODEL:3 * D_MODEL]) * y_c
    r = jnp.dot(m.astype(BF16), wout_ref[...], preferred_element_type=F32)
    o_ref[...] = x_ref[...] + _rms(r, pg_ref[...])


def _merge(x2, ys, c_in, ao, gl, w_glu, w_so, dw, dw_b, ln_g, ln_b, w_co, w_ao, w_out, post_g, length):
    n = x2.shape[0]
    tm = MERGE_ROWS
    halo_blocks = tm // CONV_HALO
    row = lambda width: pl.BlockSpec((tm, width), lambda i: (i, 0))
    dw_pad = jnp.zeros((CONV_HALO, D_CONV), F32).at[:CONV_WIDTH].set(dw)
    return pl.pallas_call(
        functools.partial(_merge_kernel, tiles_per_seq=length // tm),
        grid=(n // tm,),
        in_specs=[
            row(D_MODEL), row(D_SSM), row(2 * D_CONV),
            pl.BlockSpec((CONV_HALO, 2 * D_CONV), lambda i: (jnp.maximum(i * halo_blocks - 1, 0), 0)),
            row(D_V), row(N_BRANCH * D_MODEL),
            _resident((D_SSM, 2 * D_SSM)), _resident((D_SSM, D_MODEL)),
            _resident((CONV_HALO, D_CONV)), _resident((1, D_CONV)), _resident((1, D_CONV)),
            _resident((1, D_CONV)), _resident((D_CONV, D_MODEL)), _resident((D_V, D_MODEL)),
            _resident((D_MODEL, D_MODEL)), _resident((1, D_MODEL)),
        ],
        out_specs=row(D_MODEL),
        out_shape=jax.ShapeDtypeStruct((n, D_MODEL), F32),
        scratch_shapes=[pltpu.VMEM((CONV_HALO + tm, D_CONV), F32)],
        compiler_params=_params(("parallel",)),
        name="merge",
    )(x2, ys, c_in, c_in, ao, gl, w_glu, w_so, dw_pad, dw_b, ln_g, ln_b, w_co, w_ao, w_out, post_g)


def _ffn_kernel(x_ref, g1_ref, win_ref, wout_ref, g2_ref, o_ref):
    x = x_ref[...]
    hb = _rms(x, g1_ref[...]).astype(BF16)
    f = jnp.zeros(x.shape, F32)
    for c in range(D_FF // FFN_CHUNK):
        lo = c * FFN_CHUNK
        gate = jnp.dot(hb, win_ref[:, lo:lo + FFN_CHUNK], preferred_element_type=F32)
        up = jnp.dot(hb, win_ref[:, D_FF + lo:D_FF + lo + FFN_CHUNK], preferred_element_type=F32)
        act = (jax.nn.silu(gate) * up).astype(BF16)
        f = f + jnp.dot(act, wout_ref[lo:lo + FFN_CHUNK, :], preferred_element_type=F32)
    o_ref[...] = x + _rms(f, g2_ref[...])


def _ffn(x2, pre_g, w_in, w_out, post_g):
    n = x2.shape[0]
    tm = FFN_ROWS
    row = pl.BlockSpec((tm, D_MODEL), lambda i: (i, 0))
    return pl.pallas_call(
        _ffn_kernel,
        grid=(n // tm,),
        in_specs=[row, _resident((1, D_MODEL)), _resident((D_MODEL, 2 * D_FF)),
                  _resident((D_FF, D_MODEL)), _resident((1, D_MODEL))],
        out_specs=row,
        out_shape=jax.ShapeDtypeStruct((n, D_MODEL), F32),
        compiler_params=_params(("parallel",)),
        name="ffn",
    )(x2, pre_g, w_in, w_out, post_g)


def kernel(x, rel_bias, pre_mix_g, w_in, ssm_a_re, ssm_a_im, ssm_log_dt, ssm_b_re, ssm_b_im, ssm_c_re, ssm_c_im, ssm_d, w_ssm_glu, w_ssm_out, conv_dw, conv_dw_b, conv_ln_g, conv_ln_b, w_conv_out, lambda_q1, lambda_k1, lambda_q2, lambda_k2, attn_subln_g, w_attn_out, w_out, post_mix_g, pre_ffn_g, w_ffn_in, w_ffn_out, post_ffn_g):
    n_batch, length, _ = x.shape
    depth = w_in.shape[0]
    assert length % max(ATTN_BLOCK, MERGE_ROWS, FFN_ROWS, INPROJ_ROWS, SSM_CHUNK) == 0
    x2 = x.astype(F32).reshape(n_batch * length, D_MODEL)
    near, far = _attn_bias(rel_bias)
    vec = lambda a: a.astype(F32).reshape(1, -1)
    for layer in range(depth):
        lambda_init = 0.8 - 0.6 * math.exp(-0.3 * layer)
        lam = (jnp.exp(jnp.sum(lambda_q1[layer].astype(F32) * lambda_k1[layer].astype(F32)))
               - jnp.exp(jnp.sum(lambda_q2[layer].astype(F32) * lambda_k2[layer].astype(F32)))
               + lambda_init)
        scalars = jnp.stack([lam, jnp.asarray(1.0 - lambda_init, F32)]).astype(F32)
        u, c_in, q, k, v, gl = _inproj(x2, vec(pre_mix_g[layer]), w_in[layer].astype(BF16))
        mats = _ssm_matrices(ssm_a_re[layer].astype(F32), ssm_a_im[layer].astype(F32),
                             ssm_log_dt[layer].astype(F32), ssm_b_re[layer].astype(F32),
                             ssm_b_im[layer].astype(F32), ssm_c_re[layer].astype(F32),
                             ssm_c_im[layer].astype(F32))
        ys = _ssm(u, mats, ssm_d[layer].astype(F32), n_batch, length)
        ao = _attention(q, k, v, near, far, scalars, attn_subln_g[layer].astype(F32), n_batch, length)
        x2 = _merge(x2, ys, c_in, ao, gl, w_ssm_glu[layer].astype(BF16), w_ssm_out[layer].astype(BF16),
                    conv_dw[layer].astype(F32), vec(conv_dw_b[layer]), vec(conv_ln_g[layer]),
                    vec(conv_ln_b[layer]), w_conv_out[layer].astype(BF16), w_attn_out[layer].astype(BF16),
                    w_out[layer].astype(BF16), vec(post_mix_g[layer]), length)
        x2 = _ffn(x2, vec(pre_ffn_g[layer]), w_ffn_in[layer].astype(BF16), w_ffn_out[layer].astype(BF16),
                  vec(post_ffn_g[layer]))
    return x2.reshape(n_batch, length, D_MODEL).astype(x.dtype)
```

```python
import functools
import math

import jax
import jax.numpy as jnp
from jax import lax
from jax.experimental import pallas as pl
from jax.experimental.pallas import tpu as pltpu

F32 = jnp.float32
BF16 = jnp.bfloat16

D_MODEL = 1024
SSM_GROUP = 16
D_SSM = D_MODEL // 2
SSM_GROUPS = D_SSM // SSM_GROUP
SSM_STATE = 64
D_CONV = D_MODEL // 2
CONV_WIDTH = 31
ATTN_HEADS = 8
ATTN_HEAD_DIM = 64
ATTN_MAPS = 2 * ATTN_HEADS
D_QK = ATTN_MAPS * ATTN_HEAD_DIM
D_V = ATTN_HEADS * 2 * ATTN_HEAD_DIM
REL_BUCKETS = 32
REL_MAX_DIST = 128
D_FF = 2816
N_BRANCH = 3
OFF_SSM = 0
OFF_CONV = OFF_SSM + D_SSM
OFF_Q = OFF_CONV + 2 * D_CONV
OFF_K = OFF_Q + D_QK
OFF_V = OFF_K + D_QK
OFF_GATE = OFF_V + D_V
D_IN = OFF_GATE + N_BRANCH * D_MODEL
RMS_EPS = 1e-6
SUBLN_EPS = 1e-5
LN_EPS = 1e-5
LOG2_E = math.log2(math.e)

V7X_VMEM_LIMIT_BYTES = 56 * 1024 * 1024
SUBLANES = 8
HEAD_LANES = 2 * ATTN_HEAD_DIM

SSM_CHUNK = 32
CONV_HALO = 32
ATTN_BLOCK = 512
VT_ROWS = HEAD_LANES
INPROJ_ROWS = 256
MERGE_ROWS = 256
FFN_ROWS = 512
FFN_CHUNK = 256


def _params(sem):
    return pltpu.CompilerParams(dimension_semantics=sem, vmem_limit_bytes=V7X_VMEM_LIMIT_BYTES)


def _resident(shape):
    nd = len(shape)
    return pl.BlockSpec(shape, lambda *_: (0,) * nd, pipeline_mode=pl.Buffered(1))


def _rms(x, g):
    return x * lax.rsqrt(jnp.mean(x * x, axis=-1, keepdims=True) + RMS_EPS) * g


def _inproj_kernel(x_ref, g_ref, w_ref, wvt_ref, u_ref, c_ref, q_ref, k_ref, vt_ref, gl_ref):
    hb = _rms(x_ref[...], g_ref[...]).astype(BF16)

    def proj(lo, hi):
        return jnp.dot(hb, w_ref[:, lo:hi], preferred_element_type=F32)

    u_ref[...] = proj(OFF_SSM, OFF_CONV)
    c_ref[...] = proj(OFF_CONV, OFF_Q)
    q_ref[...] = (proj(OFF_Q, OFF_K) * (LOG2_E * ATTN_HEAD_DIM ** -0.5)).astype(BF16)
    k_ref[...] = proj(OFF_K, OFF_V).astype(BF16)
    vt = lax.dot_general(wvt_ref[...], hb, (((1,), (1,)), ((), ())), preferred_element_type=F32)
    vt_ref[...] = vt.astype(BF16).reshape(ATTN_HEADS, VT_ROWS, INPROJ_ROWS)
    for b in range(N_BRANCH):
        lo = OFF_GATE + b * D_MODEL
        gl_ref[:, b * D_MODEL:(b + 1) * D_MODEL] = proj(lo, lo + D_MODEL)


def _inproj(x2, g, w, n_batch, length):
    n = x2.shape[0]
    tm = INPROJ_ROWS
    tiles = length // tm
    row = lambda width: pl.BlockSpec((tm, width), lambda i: (i, 0))
    wvt = w[:, OFF_V:OFF_GATE].T
    return pl.pallas_call(
        _inproj_kernel,
        grid=(n // tm,),
        in_specs=[row(D_MODEL), _resident((1, D_MODEL)), _resident((D_MODEL, D_IN)),
                  _resident((D_V, D_MODEL))],
        out_specs=[row(D_SSM), row(2 * D_CONV), row(D_QK), row(D_QK),
                   pl.BlockSpec((None, ATTN_HEADS, None, VT_ROWS, tm),
                                lambda i: (i // tiles, 0, i % tiles, 0, 0)),
                   row(N_BRANCH * D_MODEL)],
        out_shape=[
            jax.ShapeDtypeStruct((n, D_SSM), F32),
            jax.ShapeDtypeStruct((n, 2 * D_CONV), F32),
            jax.ShapeDtypeStruct((n, D_QK), BF16),
            jax.ShapeDtypeStruct((n, D_QK), BF16),
            jax.ShapeDtypeStruct((n_batch, ATTN_HEADS, tiles, VT_ROWS, tm), BF16),
            jax.ShapeDtypeStruct((n, N_BRANCH * D_MODEL), F32),
        ],
        compiler_params=_params(("parallel",)),
        name="inproj",
    )(x2, g, w, wvt)


def _ssm_matrices(a_re, a_im, log_dt, b_re, b_im, c_re, c_im):
    hp = lax.Precision.HIGHEST
    t_len = SSM_CHUNK
    dt = jnp.exp(log_dt)[:, None]
    zr, zi = a_re * dt, a_im * dt
    er = jnp.exp(zr)
    lbr, lbi = er * jnp.cos(zi), er * jnp.sin(zi)
    den = a_re * a_re + a_im * a_im
    nr, ni = lbr - 1.0, lbi
    cr, ci = (nr * a_re + ni * a_im) / den, (ni * a_re - nr * a_im) / den
    bbr = cr[..., None] * b_re - ci[..., None] * b_im
    bbi = cr[..., None] * b_im + ci[..., None] * b_re
    tau = jnp.arange(t_len + 1, dtype=F32)[:, None, None]
    pe = jnp.exp(tau * zr)
    pwr, pwi = pe * jnp.cos(tau * zi), pe * jnp.sin(tau * zi)
    e_r = pwr[:t_len, :, :, None] * bbr - pwi[:t_len, :, :, None] * bbi
    e_i = pwr[:t_len, :, :, None] * bbi + pwi[:t_len, :, :, None] * bbr
    kern = (jnp.einsum("gcp,tgpd->tgcd", c_re, e_r, precision=hp)
            - jnp.einsum("gcp,tgpd->tgcd", c_im, e_i, precision=hp))
    lag = jnp.arange(t_len)[None, :] - jnp.arange(t_len)[:, None]
    toe = jnp.where((lag >= 0)[:, :, None, None, None], kern[jnp.clip(lag, 0, t_len - 1)], 0.0)
    gc = SSM_GROUP * t_len
    w_intra = toe.transpose(2, 0, 4, 1, 3).reshape(SSM_GROUPS, gc, gc)
    w_sr = e_r[::-1].transpose(1, 0, 3, 2).reshape(SSM_GROUPS, gc, SSM_STATE)
    w_si = e_i[::-1].transpose(1, 0, 3, 2).reshape(SSM_GROUPS, gc, SSM_STATE)
    p1r, p1i = pwr[1:, :, None, :], pwi[1:, :, None, :]
    w_or = (c_re * p1r - c_im * p1i).transpose(1, 3, 0, 2).reshape(SSM_GROUPS, SSM_STATE, gc)
    w_oi = (-(c_re * p1i + c_im * p1r)).transpose(1, 3, 0, 2).reshape(SSM_GROUPS, SSM_STATE, gc)
    a_r = pwr[t_len][:, None, :]
    a_i = pwi[t_len][:, None, :]
    return (w_intra.astype(BF16), w_sr.astype(BF16), w_si.astype(BF16),
            w_or.astype(BF16), w_oi.astype(BF16), a_r, a_i)


def _ssm_kernel(u_ref, wi_ref, wsr_ref, wsi_ref, wor_ref, woi_ref, ar_ref, ai_ref, d_ref,
                o_ref, sr_ref, si_ref, xr_ref, xi_ref, *, n_batch, n_chunk):
    u = u_ref[0]
    ub = u.astype(BF16)
    sr_ref[...] = jnp.dot(ub, wsr_ref[0], preferred_element_type=F32)
    si_ref[...] = jnp.dot(ub, wsi_ref[0], preferred_element_type=F32)
    a_r, a_i = ar_ref[0], ai_ref[0]

    def step(c, carry):
        nxt = []
        for b in range(n_batch):
            x_r, x_i = carry[2 * b], carry[2 * b + 1]
            row = pl.ds(b * n_chunk + c, 1)
            xr_ref[row, :] = x_r
            xi_ref[row, :] = x_i
            nxt.append(a_r * x_r - a_i * x_i + sr_ref[row, :])
            nxt.append(a_r * x_i + a_i * x_r + si_ref[row, :])
        return tuple(nxt)

    zero = jnp.zeros((1, SSM_STATE), F32)
    lax.fori_loop(0, n_chunk, step, (zero,) * (2 * n_batch))
    y = (jnp.dot(ub, wi_ref[0], preferred_element_type=F32)
         + jnp.dot(xr_ref[...].astype(BF16), wor_ref[0], preferred_element_type=F32)
         + jnp.dot(xi_ref[...].astype(BF16), woi_ref[0], preferred_element_type=F32)
         + d_ref[0] * u)
    o_ref[0] = jax.nn.gelu(y, approximate=True).astype(BF16)


def _ssm(u2, mats, d_skip, n_batch, length):
    t_len = SSM_CHUNK
    n_chunk = length // t_len
    rows = n_batch * n_chunk
    gc = SSM_GROUP * t_len
    ug = (u2.reshape(rows, t_len, SSM_GROUPS, SSM_GROUP).transpose(2, 0, 1, 3)
          .reshape(SSM_GROUPS, rows, gc))
    d_t = jnp.tile(d_skip[:, None, :], (1, t_len, 1)).reshape(SSM_GROUPS, 1, gc)
    w_intra, w_sr, w_si, w_or, w_oi, a_r, a_i = mats
    grp = lambda *shape: pl.BlockSpec((1,) + shape, lambda g: (g, 0, 0))
    yg = pl.pallas_call(
        functools.partial(_ssm_kernel, n_batch=n_batch, n_chunk=n_chunk),
        grid=(SSM_GROUPS,),
        in_specs=[grp(rows, gc), grp(gc, gc), grp(gc, SSM_STATE), grp(gc, SSM_STATE),
                  grp(SSM_STATE, gc), grp(SSM_STATE, gc), grp(1, SSM_STATE), grp(1, SSM_STATE),
                  grp(1, gc)],
        out_specs=grp(rows, gc),
        out_shape=jax.ShapeDtypeStruct((SSM_GROUPS, rows, gc), BF16),
        scratch_shapes=[pltpu.VMEM((rows, SSM_STATE), F32)] * 4,
        compiler_params=_params(("parallel",)),
        name="ssm",
    )(ug, w_intra, w_sr, w_si, w_or, w_oi, a_r, a_i, d_t)
    return (yg.reshape(SSM_GROUPS, rows, t_len, SSM_GROUP).transpose(1, 2, 0, 3)
            .reshape(n_batch * length, D_SSM))


def _rel_bucket(rel):
    n = jnp.maximum(rel, 0)
    max_exact = REL_BUCKETS // 2
    nf = jnp.maximum(n, max_exact).astype(F32)
    large = max_exact + (jnp.log(nf / max_exact) / math.log(REL_MAX_DIST / max_exact)
                         * (REL_BUCKETS - max_exact)).astype(jnp.int32)
    large = jnp.minimum(large, REL_BUCKETS - 1)
    return jnp.where(n < max_exact, n, large)


def _attn_bias(rel_bias):
    blk = ATTN_BLOCK
    assert blk + 1 >= REL_MAX_DIST
    rb = rel_bias.astype(F32) * LOG2_E
    rel = jnp.arange(-blk, 2 * blk + 1)
    table = jnp.where((rel >= 0)[None, :], rb[_rel_bucket(rel)].T, -jnp.inf)

    def toeplitz(v):
        rows = jnp.broadcast_to(v[:, None, :], (ATTN_MAPS, blk, 2 * blk)).reshape(ATTN_MAPS, -1)
        skew = rows[:, :blk * (2 * blk - 1)].reshape(ATTN_MAPS, blk, 2 * blk - 1)
        return skew[:, :, blk - 1:]

    b_diag = toeplitz(table[:, 1:2 * blk + 1])
    b_prev = toeplitz(table[:, blk + 1:])
    near = jnp.stack([b_prev, b_diag], axis=1)
    near = near.reshape(ATTN_HEADS, 2, 2, blk, blk).transpose(0, 2, 3, 1, 4)
    near = near.reshape(ATTN_HEADS, 2, blk, 2 * blk)
    far = rb[_rel_bucket(jnp.full((), blk + 1, jnp.int32))]
    far = jnp.broadcast_to(far.reshape(ATTN_HEADS, 2, 1), (ATTN_HEADS, 2, blk))
    return near, far.reshape(ATTN_HEADS, 1, 2 * blk)


def _attn_kernel(sc_ref, q_ref, k_ref, vt_ref, near_ref, far_ref, g_ref, o_ref,
                 qs_ref, sa_ref, sb_ref, m_ref, l_ref, acc_ref):
    blk = ATTN_BLOCK
    n_sub = blk // INPROJ_ROWS
    i = pl.program_id(2)
    q = q_ref[...]
    lane = lax.broadcasted_iota(jnp.int32, q.shape, 1)
    zero = jnp.zeros_like(q)
    qs_ref[0:blk, :] = jnp.where(lane < ATTN_HEAD_DIM, q, zero)
    qs_ref[blk:, :] = jnp.where(lane >= ATTN_HEAD_DIM, q, zero)
    m_ref[...] = jnp.full(m_ref.shape, -jnp.inf, F32)
    l_ref[...] = jnp.zeros(l_ref.shape, F32)
    acc_ref[...] = jnp.zeros(acc_ref.shape, F32)

    def scores(j, dst_ref):
        j = jnp.maximum(j, 0)
        kb = k_ref[pl.ds(pl.multiple_of(j * blk, blk), blk), :]
        dst_ref[...] = lax.dot_general(kb, qs_ref[...], (((1,), (1,)), ((), ())),
                                       preferred_element_type=F32)

    def softmax_pv(j, src_ref, bias, far):
        s = src_ref[...]
        if bias is not None:
            s = s + bias
        cm = jnp.max(s, axis=0, keepdims=True)
        if far is not None:
            cm = cm + far
        m_old = m_ref[...]
        m_new = jnp.maximum(m_old, cm)
        shift = m_new if far is None else m_new - far
        p = jnp.exp2(s - shift)
        alpha = jnp.exp2(m_old - m_new)
        l_ref[...] = alpha * l_ref[...] + jnp.sum(p, axis=0, keepdims=True)
        vt = jnp.concatenate([vt_ref[j * n_sub + r] for r in range(n_sub)], axis=1)
        acc_ref[...] = alpha * acc_ref[...] + jnp.dot(vt, p.astype(BF16), preferred_element_type=F32)
        m_ref[...] = m_new

    far = far_ref[0]
    scores(i, sa_ref)
    scores(i - 1, sb_ref)
    softmax_pv(i, sa_ref, near_ref[0, 1], None)

    @pl.when(i >= 1)
    def _():
        scores(i - 2, sa_ref)
        softmax_pv(i - 1, sb_ref, near_ref[0, 0], None)

    n_far = jnp.maximum(i - 1, 0)

    def far_pair(t, carry):
        j = i - 2 - 2 * t
        scores(j - 1, sb_ref)
        softmax_pv(j, sa_ref, None, far)
        scores(j - 2, sa_ref)
        softmax_pv(j - 1, sb_ref, None, far)
        return carry

    lax.fori_loop(0, n_far // 2, far_pair, 0)

    @pl.when(n_far % 2 == 1)
    def _():
        softmax_pv(0, sa_ref, None, far)

    lam, out_scale = sc_ref[0], sc_ref[1]
    a = acc_ref[...] * (1.0 / l_ref[...])
    o = a[:, :blk] - lam * a[:, blk:]
    o = o * lax.rsqrt(jnp.mean(o * o, axis=0, keepdims=True) + SUBLN_EPS) * g_ref[...] * out_scale
    o_ref[...] = o.T.astype(BF16)


def _attention(q2, k2, vt, near, far, scalars, subln_g, n_batch, length):
    blk = ATTN_BLOCK
    nq = length // blk
    n = n_batch * length
    return pl.pallas_call(
        _attn_kernel,
        grid=(n_batch, ATTN_HEADS, nq),
        in_specs=[
            pl.BlockSpec(memory_space=pltpu.SMEM),
            pl.BlockSpec((blk, HEAD_LANES), lambda b, h, i: (b * nq + i, h)),
            pl.BlockSpec((length, HEAD_LANES), lambda b, h, i: (b, h)),
            pl.BlockSpec((None, None, length // INPROJ_ROWS, VT_ROWS, INPROJ_ROWS),
                         lambda b, h, i: (b, h, 0, 0, 0)),
            pl.BlockSpec((1, 2, blk, 2 * blk), lambda b, h, i: (h, 0, 0, 0)),
            pl.BlockSpec((1, 1, 2 * blk), lambda b, h, i: (h, 0, 0)),
            pl.BlockSpec((HEAD_LANES, 1), lambda b, h, i: (0, 0)),
        ],
        out_specs=pl.BlockSpec((blk, HEAD_LANES), lambda b, h, i: (b * nq + i, h)),
        out_shape=jax.ShapeDtypeStruct((n, D_V), BF16),
        scratch_shapes=[
            pltpu.VMEM((2 * blk, HEAD_LANES), BF16),
            pltpu.VMEM((blk, 2 * blk), F32),
            pltpu.VMEM((blk, 2 * blk), F32),
            pltpu.VMEM((1, 2 * blk), F32),
            pltpu.VMEM((1, 2 * blk), F32),
            pltpu.VMEM((VT_ROWS, 2 * blk), F32),
        ],
        compiler_params=_params(("parallel", "parallel", "arbitrary")),
        name="diff_attention",
    )(scalars, q2, k2, vt, near, far, subln_g.reshape(HEAD_LANES, 1))


def _merge_kernel(x_ref, ys_ref, cc_ref, cp_ref, ao_ref, gl_ref, wglu_ref, wso_ref, dw_ref, dwb_ref,
                  lng_ref, lnb_ref, wco_ref, wao_ref, wout_ref, pg_ref, o_ref, gbuf_ref, gsh_ref,
                  *, tiles_per_seq):
    tm = MERGE_ROWS
    i = pl.program_id(0)
    z = jnp.dot(ys_ref[...], wglu_ref[...], preferred_element_type=F32)
    glu = (z[:, :D_SSM] * jax.nn.sigmoid(z[:, D_SSM:])).astype(BF16)
    y_a = jnp.dot(glu, wso_ref[...], preferred_element_type=F32)
    m = jax.nn.sigmoid(gl_ref[:, 0:D_MODEL]) * y_a
    cc = cc_ref[...]
    cp = cp_ref[...]
    g_prev = cp[:, :D_CONV] * jax.nn.sigmoid(cp[:, D_CONV:])
    g_prev = jnp.where(i % tiles_per_seq == 0, jnp.zeros_like(g_prev), g_prev)
    gbuf_ref[0:CONV_HALO, :] = g_prev
    gbuf_ref[CONV_HALO:, :] = cc[:, :D_CONV] * jax.nn.sigmoid(cc[:, D_CONV:])
    shifted_rows = tm + CONV_HALO - SUBLANES
    for r in range(1, SUBLANES):
        gsh_ref[r - 1] = gbuf_ref[r:r + shifted_rows, :]
    y = jnp.zeros((tm, D_CONV), F32) + dwb_ref[...]
    base = CONV_HALO - (CONV_WIDTH - 1)
    for tap in range(CONV_WIDTH):
        r, lo = (base + tap) % SUBLANES, (base + tap) // SUBLANES * SUBLANES
        win = gbuf_ref[lo:lo + tm, :] if r == 0 else gsh_ref[r - 1, lo:lo + tm, :]
        y = y + win * dw_ref[tap:tap + 1, :]
    mu = jnp.mean(y, axis=-1, keepdims=True)
    yc = y - mu
    var = jnp.mean(yc * yc, axis=-1, keepdims=True)
    yn = yc * lax.rsqrt(var + LN_EPS) * lng_ref[...] + lnb_ref[...]
    y_b = jnp.dot(jax.nn.silu(yn).astype(BF16), wco_ref[...], preferred_element_type=F32)
    m = m + jax.nn.sigmoid(gl_ref[:, D_MODEL:2 * D_MODEL]) * y_b
    y_c = jnp.dot(ao_ref[...], wao_ref[...], preferred_element_type=F32)
    m = m + jax.nn.sigmoid(gl_ref[:, 2 * D_MODEL:3 * D_MODEL]) * y_c
    r = jnp.dot(m.astype(BF16), wout_ref[...], preferred_element_type=F32)
    o_ref[...] = x_ref[...] + _rms(r, pg_ref[...])


def _merge(x2, ys, c_in, ao, gl, w_glu, w_so, dw, dw_b, ln_g, ln_b, w_co, w_ao, w_out, post_g, length):
    n = x2.shape[0]
    tm = MERGE_ROWS
    halo_blocks = tm // CONV_HALO
    row = lambda width: pl.BlockSpec((tm, width), lambda i: (i, 0))
    dw_pad = jnp.zeros((CONV_HALO, D_CONV), F32).at[:CONV_WIDTH].set(dw)
    return pl.pallas_call(
        functools.partial(_merge_kernel, tiles_per_seq=length // tm),
        grid=(n // tm,),
        in_specs=[
            row(D_MODEL), row(D_SSM), row(2 * D_CONV),
            pl.BlockSpec((CONV_HALO, 2 * D_CONV), lambda i: (jnp.maximum(i * halo_blocks - 1, 0), 0)),
            row(D_V), row(N_BRANCH * D_MODEL),
            _resident((D_SSM, 2 * D_SSM)), _resident((D_SSM, D_MODEL)),
            _resident((CONV_HALO, D_CONV)), _resident((1, D_CONV)), _resident((1, D_CONV)),
            _resident((1, D_CONV)), _resident((D_CONV, D_MODEL)), _resident((D_V, D_MODEL)),
            _resident((D_MODEL, D_MODEL)), _resident((1, D_MODEL)),
        ],
        out_specs=row(D_MODEL),
        out_shape=jax.ShapeDtypeStruct((n, D_MODEL), F32),
        scratch_shapes=[pltpu.VMEM((CONV_HALO + tm, D_CONV), F32),
                        pltpu.VMEM((SUBLANES - 1, CONV_HALO + tm - SUBLANES, D_CONV), F32)],
        compiler_params=_params(("parallel",)),
        name="merge",
    )(x2, ys, c_in, c_in, ao, gl, w_glu, w_so, dw_pad, dw_b, ln_g, ln_b, w_co, w_ao, w_out, post_g)


def _ffn_kernel(x_ref, g1_ref, win_ref, wout_ref, g2_ref, o_ref):
    x = x_ref[...]
    hb = _rms(x, g1_ref[...]).astype(BF16)
    f = jnp.zeros(x.shape, F32)
    for c in range(D_FF // FFN_CHUNK):
        lo = c * FFN_CHUNK
        gate = jnp.dot(hb, win_ref[:, lo:lo + FFN_CHUNK], preferred_element_type=F32)
        up = jnp.dot(hb, win_ref[:, D_FF + lo:D_FF + lo + FFN_CHUNK], preferred_element_type=F32)
        act = (jax.nn.silu(gate) * up).astype(BF16)
        f = f + jnp.dot(act, wout_ref[lo:lo + FFN_CHUNK, :], preferred_element_type=F32)
    o_ref[...] = x + _rms(f, g2_ref[...])


def _ffn(x2, pre_g, w_in, w_out, post_g):
    n = x2.shape[0]
    tm = FFN_ROWS
    row = pl.BlockSpec((tm, D_MODEL), lambda i: (i, 0))
    return pl.pallas_call(
        _ffn_kernel,
        grid=(n // tm,),
        in_specs=[row, _resident((1, D_MODEL)), _resident((D_MODEL, 2 * D_FF)),
                  _resident((D_FF, D_MODEL)), _resident((1, D_MODEL))],
        out_specs=row,
        out_shape=jax.ShapeDtypeStruct((n, D_MODEL), F32),
        compiler_params=_params(("parallel",)),
        name="ffn",
    )(x2, pre_g, w_in, w_out, post_g)


def kernel(x, rel_bias, pre_mix_g, w_in, ssm_a_re, ssm_a_im, ssm_log_dt, ssm_b_re, ssm_b_im, ssm_c_re, ssm_c_im, ssm_d, w_ssm_glu, w_ssm_out, conv_dw, conv_dw_b, conv_ln_g, conv_ln_b, w_conv_out, lambda_q1, lambda_k1, lambda_q2, lambda_k2, attn_subln_g, w_attn_out, w_out, post_mix_g, pre_ffn_g, w_ffn_in, w_ffn_out, post_ffn_g):
    n_batch, length, _ = x.shape
    depth = w_in.shape[0]
    assert length % max(ATTN_BLOCK, MERGE_ROWS, FFN_ROWS, INPROJ_ROWS, SSM_CHUNK) == 0
    x2 = x.astype(F32).reshape(n_batch * length, D_MODEL)
    near, far = _attn_bias(rel_bias)
    vec = lambda a: a.astype(F32).reshape(1, -1)
    for layer in range(depth):
        lambda_init = 0.8 - 0.6 * math.exp(-0.3 * layer)
        lam = (jnp.exp(jnp.sum(lambda_q1[layer].astype(F32) * lambda_k1[layer].astype(F32)))
               - jnp.exp(jnp.sum(lambda_q2[layer].astype(F32) * lambda_k2[layer].astype(F32)))
               + lambda_init)
        scalars = jnp.stack([lam, jnp.asarray(1.0 - lambda_init, F32)]).astype(F32)
        u, c_in, q, k, vt, gl = _inproj(x2, vec(pre_mix_g[layer]), w_in[layer].astype(BF16), n_batch, length)
        mats = _ssm_matrices(ssm_a_re[layer].astype(F32), ssm_a_im[layer].astype(F32),
                             ssm_log_dt[layer].astype(F32), ssm_b_re[layer].astype(F32),
                             ssm_b_im[layer].astype(F32), ssm_c_re[layer].astype(F32),
                             ssm_c_im[layer].astype(F32))
        ys = _ssm(u, mats, ssm_d[layer].astype(F32), n_batch, length)
        ao = _attention(q, k, vt, near, far, scalars, attn_subln_g[layer].astype(F32), n_batch, length)
        x2 = _merge(x2, ys, c_in, ao, gl, w_ssm_glu[layer].astype(BF16), w_ssm_out[layer].astype(BF16),
                    conv_dw[layer].astype(F32), vec(conv_dw_b[layer]), vec(conv_ln_g[layer]),
                    vec(conv_ln_b[layer]), w_conv_out[layer].astype(BF16), w_attn_out[layer].astype(BF16),
                    w_out[layer].astype(BF16), vec(post_mix_g[layer]), length)
        x2 = _ffn(x2, vec(pre_ffn_g[layer]), w_ffn_in[layer].astype(BF16), w_ffn_out[layer].astype(BF16),
                  vec(post_ffn_g[layer]))
    return x2.reshape(n_batch, length, D_MODEL).astype(x.dtype)
```

```python
import functools
import math

import jax
import jax.numpy as jnp
from jax import lax
from jax.experimental import pallas as pl
from jax.experimental.pallas import tpu as pltpu

F32 = jnp.float32
BF16 = jnp.bfloat16

D_MODEL = 1024
SSM_GROUP = 16
D_SSM = D_MODEL // 2
SSM_GROUPS = D_SSM // SSM_GROUP
SSM_STATE = 64
D_CONV = D_MODEL // 2
CONV_WIDTH = 31
ATTN_HEADS = 8
ATTN_HEAD_DIM = 64
ATTN_MAPS = 2 * ATTN_HEADS
D_QK = ATTN_MAPS * ATTN_HEAD_DIM
D_V = ATTN_HEADS * 2 * ATTN_HEAD_DIM
REL_BUCKETS = 32
REL_MAX_DIST = 128
D_FF = 2816
N_BRANCH = 3
OFF_SSM = 0
OFF_CONV = OFF_SSM + D_SSM
OFF_Q = OFF_CONV + 2 * D_CONV
OFF_K = OFF_Q + D_QK
OFF_V = OFF_K + D_QK
OFF_GATE = OFF_V + D_V
D_IN = OFF_GATE + N_BRANCH * D_MODEL
RMS_EPS = 1e-6
SUBLN_EPS = 1e-5
LN_EPS = 1e-5
LOG2_E = math.log2(math.e)

V7X_VMEM_LIMIT_BYTES = 56 * 1024 * 1024
SUBLANES = 8
HEAD_LANES = 2 * ATTN_HEAD_DIM

SSM_CHUNK = 32
CONV_HALO = 32
ATTN_BLOCK = 512
VT_ROWS = HEAD_LANES
INPROJ_ROWS = 256
MERGE_ROWS = 256
FFN_ROWS = 512
FFN_CHUNK = 256


def _params(sem):
    return pltpu.CompilerParams(dimension_semantics=sem, vmem_limit_bytes=V7X_VMEM_LIMIT_BYTES)


def _resident(shape):
    nd = len(shape)
    return pl.BlockSpec(shape, lambda *_: (0,) * nd, pipeline_mode=pl.Buffered(1))


def _rms(x, g):
    return x * lax.rsqrt(jnp.mean(x * x, axis=-1, keepdims=True) + RMS_EPS) * g


def _inproj_kernel(x_ref, g_ref, w_ref, wvt_ref, u_ref, c_ref, q_ref, k_ref, vt_ref, gl_ref):
    hb = _rms(x_ref[...], g_ref[...]).astype(BF16)

    def proj(lo, hi):
        return jnp.dot(hb, w_ref[:, lo:hi], preferred_element_type=F32)

    u_ref[...] = proj(OFF_SSM, OFF_CONV).astype(BF16)
    c_ref[...] = proj(OFF_CONV, OFF_Q)
    q_ref[...] = (proj(OFF_Q, OFF_K) * (LOG2_E * ATTN_HEAD_DIM ** -0.5)).astype(BF16)
    k_ref[...] = proj(OFF_K, OFF_V).astype(BF16)
    vt = lax.dot_general(wvt_ref[...], hb, (((1,), (1,)), ((), ())), preferred_element_type=F32)
    vt_ref[...] = vt.astype(BF16).reshape(ATTN_HEADS, VT_ROWS, INPROJ_ROWS)
    for b in range(N_BRANCH):
        lo = OFF_GATE + b * D_MODEL
        gl_ref[:, b * D_MODEL:(b + 1) * D_MODEL] = proj(lo, lo + D_MODEL)


def _inproj(x2, g, w, wvt, n_batch, length):
    n = x2.shape[0]
    tm = INPROJ_ROWS
    tiles = length // tm
    row = lambda width: pl.BlockSpec((tm, width), lambda i: (i, 0))
    return pl.pallas_call(
        _inproj_kernel,
        grid=(n // tm,),
        in_specs=[row(D_MODEL), _resident((1, D_MODEL)), _resident((D_MODEL, D_IN)),
                  _resident((D_V, D_MODEL))],
        out_specs=[row(D_SSM), row(2 * D_CONV), row(D_QK), row(D_QK),
                   pl.BlockSpec((None, ATTN_HEADS, None, VT_ROWS, tm),
                                lambda i: (i // tiles, 0, i % tiles, 0, 0)),
                   row(N_BRANCH * D_MODEL)],
        out_shape=[
            jax.ShapeDtypeStruct((n, D_SSM), BF16),
            jax.ShapeDtypeStruct((n, 2 * D_CONV), F32),
            jax.ShapeDtypeStruct((n, D_QK), BF16),
            jax.ShapeDtypeStruct((n, D_QK), BF16),
            jax.ShapeDtypeStruct((n_batch, ATTN_HEADS, tiles, VT_ROWS, tm), BF16),
            jax.ShapeDtypeStruct((n, N_BRANCH * D_MODEL), F32),
        ],
        compiler_params=_params(("parallel",)),
        name="inproj",
    )(x2, g, w, wvt)


def _ssm_matrices(a_re, a_im, log_dt, b_re, b_im, c_re, c_im):
    hp = lax.Precision.HIGHEST
    t_len = SSM_CHUNK
    dt = jnp.exp(log_dt)[:, None]
    zr, zi = a_re * dt, a_im * dt
    er = jnp.exp(zr)
    lbr, lbi = er * jnp.cos(zi), er * jnp.sin(zi)
    den = a_re * a_re + a_im * a_im
    nr, ni = lbr - 1.0, lbi
    cr, ci = (nr * a_re + ni * a_im) / den, (ni * a_re - nr * a_im) / den
    bbr = cr[..., None] * b_re - ci[..., None] * b_im
    bbi = cr[..., None] * b_im + ci[..., None] * b_re
    tau = jnp.arange(t_len + 1, dtype=F32)[:, None, None]
    pe = jnp.exp(tau * zr)
    pwr, pwi = pe * jnp.cos(tau * zi), pe * jnp.sin(tau * zi)
    e_r = pwr[:t_len, :, :, None] * bbr - pwi[:t_len, :, :, None] * bbi
    e_i = pwr[:t_len, :, :, None] * bbi + pwi[:t_len, :, :, None] * bbr
    kern = (jnp.einsum("gcp,tgpd->tgcd", c_re, e_r, precision=hp)
            - jnp.einsum("gcp,tgpd->tgcd", c_im, e_i, precision=hp))
    lag = jnp.arange(t_len)[None, :] - jnp.arange(t_len)[:, None]
    pick = (lag[:, :, None] == jnp.arange(t_len)[None, None, :]).astype(BF16)
    gc = SSM_GROUP * t_len
    w_intra = jnp.einsum("stu,ugcd->gsdtc", pick, kern.astype(BF16),
                         preferred_element_type=F32).astype(BF16).reshape(SSM_GROUPS, gc, gc)
    w_sr = e_r[::-1].transpose(1, 0, 3, 2).reshape(SSM_GROUPS, gc, SSM_STATE)
    w_si = e_i[::-1].transpose(1, 0, 3, 2).reshape(SSM_GROUPS, gc, SSM_STATE)
    p1r, p1i = pwr[1:, :, None, :], pwi[1:, :, None, :]
    w_or = (c_re * p1r - c_im * p1i).transpose(1, 3, 0, 2).reshape(SSM_GROUPS, SSM_STATE, gc)
    w_oi = (-(c_re * p1i + c_im * p1r)).transpose(1, 3, 0, 2).reshape(SSM_GROUPS, SSM_STATE, gc)
    a_r = pwr[t_len][:, None, :]
    a_i = pwi[t_len][:, None, :]
    return (w_intra, w_sr.astype(BF16), w_si.astype(BF16),
            w_or.astype(BF16), w_oi.astype(BF16), a_r, a_i)


def _ssm_kernel(u_ref, wi_ref, wsr_ref, wsi_ref, wor_ref, woi_ref, ar_ref, ai_ref, d_ref,
                o_ref, sr_ref, si_ref, xr_ref, xi_ref, *, n_batch, n_chunk):
    ub = u_ref[0]
    sr_ref[...] = jnp.dot(ub, wsr_ref[0], preferred_element_type=F32)
    si_ref[...] = jnp.dot(ub, wsi_ref[0], preferred_element_type=F32)
    a_r, a_i = ar_ref[0], ai_ref[0]

    def step(c, carry):
        nxt = []
        for b in range(n_batch):
            x_r, x_i = carry[2 * b], carry[2 * b + 1]
            row = pl.ds(b * n_chunk + c, 1)
            xr_ref[row, :] = x_r
            xi_ref[row, :] = x_i
            nxt.append(a_r * x_r - a_i * x_i + sr_ref[row, :])
            nxt.append(a_r * x_i + a_i * x_r + si_ref[row, :])
        return tuple(nxt)

    zero = jnp.zeros((1, SSM_STATE), F32)
    lax.fori_loop(0, n_chunk, step, (zero,) * (2 * n_batch))
    y = (jnp.dot(ub, wi_ref[0], preferred_element_type=F32)
         + jnp.dot(xr_ref[...].astype(BF16), wor_ref[0], preferred_element_type=F32)
         + jnp.dot(xi_ref[...].astype(BF16), woi_ref[0], preferred_element_type=F32)
         + d_ref[0] * ub.astype(F32))
    o_ref[0] = jax.nn.gelu(y, approximate=True).astype(BF16)


def _ssm(u2, mats, d_skip, n_batch, length):
    t_len = SSM_CHUNK
    n_chunk = length // t_len
    rows = n_batch * n_chunk
    gc = SSM_GROUP * t_len
    ug = (u2.reshape(rows, t_len, SSM_GROUPS, SSM_GROUP).transpose(2, 0, 1, 3)
          .reshape(SSM_GROUPS, rows, gc))
    d_t = jnp.tile(d_skip[:, None, :], (1, t_len, 1)).reshape(SSM_GROUPS, 1, gc)
    w_intra, w_sr, w_si, w_or, w_oi, a_r, a_i = mats
    grp = lambda *shape: pl.BlockSpec((1,) + shape, lambda g: (g, 0, 0))
    yg = pl.pallas_call(
        functools.partial(_ssm_kernel, n_batch=n_batch, n_chunk=n_chunk),
        grid=(SSM_GROUPS,),
        in_specs=[grp(rows, gc), grp(gc, gc), grp(gc, SSM_STATE), grp(gc, SSM_STATE),
                  grp(SSM_STATE, gc), grp(SSM_STATE, gc), grp(1, SSM_STATE), grp(1, SSM_STATE),
                  grp(1, gc)],
        out_specs=grp(rows, gc),
        out_shape=jax.ShapeDtypeStruct((SSM_GROUPS, rows, gc), BF16),
        scratch_shapes=[pltpu.VMEM((rows, SSM_STATE), F32)] * 4,
        compiler_params=_params(("parallel",)),
        name="ssm",
    )(ug, w_intra, w_sr, w_si, w_or, w_oi, a_r, a_i, d_t)
    return (yg.reshape(SSM_GROUPS, rows, t_len, SSM_GROUP).transpose(1, 2, 0, 3)
            .reshape(n_batch * length, D_SSM))


def _rel_bucket(rel):
    n = jnp.maximum(rel, 0)
    max_exact = REL_BUCKETS // 2
    nf = jnp.maximum(n, max_exact).astype(F32)
    large = max_exact + (jnp.log(nf / max_exact) / math.log(REL_MAX_DIST / max_exact)
                         * (REL_BUCKETS - max_exact)).astype(jnp.int32)
    large = jnp.minimum(large, REL_BUCKETS - 1)
    return jnp.where(n < max_exact, n, large)


def _attn_bias(rel_bias):
    blk = ATTN_BLOCK
    assert blk + 1 >= REL_MAX_DIST
    rb = rel_bias.astype(F32) * LOG2_E
    rel = jnp.arange(-blk, 2 * blk + 1)
    table = jnp.where((rel >= 0)[None, :], rb[_rel_bucket(rel)].T, -jnp.inf)

    def toeplitz(v):
        rows = jnp.broadcast_to(v[:, None, :], (ATTN_MAPS, blk, 2 * blk)).reshape(ATTN_MAPS, -1)
        skew = rows[:, :blk * (2 * blk - 1)].reshape(ATTN_MAPS, blk, 2 * blk - 1)
        return skew[:, :, blk - 1:]

    b_diag = toeplitz(table[:, 1:2 * blk + 1])
    b_prev = toeplitz(table[:, blk + 1:])
    near = jnp.stack([b_prev, b_diag], axis=1)
    near = near.reshape(ATTN_HEADS, 2, 2, blk, blk).transpose(0, 2, 3, 1, 4)
    near = near.reshape(ATTN_HEADS, 2, blk, 2 * blk)
    far = rb[_rel_bucket(jnp.full((), blk + 1, jnp.int32))]
    far = jnp.broadcast_to(far.reshape(ATTN_HEADS, 2, 1), (ATTN_HEADS, 2, blk))
    return near, far.reshape(ATTN_HEADS, 1, 2 * blk)


def _attn_kernel(sc_ref, q_ref, k_ref, vt_ref, near_ref, far_ref, g_ref, o_ref,
                 qs_ref, sa_ref, sb_ref, m_ref, l_ref, acc_ref):
    blk = ATTN_BLOCK
    n_sub = blk // INPROJ_ROWS
    i = pl.program_id(2)
    q = q_ref[...]
    lane = lax.broadcasted_iota(jnp.int32, q.shape, 1)
    zero = jnp.zeros_like(q)
    qs_ref[0:blk, :] = jnp.where(lane < ATTN_HEAD_DIM, q, zero)
    qs_ref[blk:, :] = jnp.where(lane >= ATTN_HEAD_DIM, q, zero)

    def scores(j, dst_ref):
        j = jnp.maximum(j, 0)
        kb = k_ref[pl.ds(pl.multiple_of(j * blk, blk), blk), :]
        dst_ref[...] = lax.dot_general(kb, qs_ref[...], (((1,), (1,)), ((), ())),
                                       preferred_element_type=F32)

    def v_tile(j):
        return jnp.concatenate([vt_ref[j * n_sub + r] for r in range(n_sub)], axis=1)

    def first_step(j, src_ref, bias):
        s = src_ref[...] + bias
        m_new = jnp.max(s, axis=0, keepdims=True)
        p = jnp.exp2(s - m_new)
        l_ref[...] = jnp.sum(p, axis=0, keepdims=True)
        acc_ref[...] = jnp.dot(v_tile(j), p.astype(BF16), preferred_element_type=F32)
        m_ref[...] = m_new

    def softmax_pv(j, src_ref, bias=None, far=None):
        s = src_ref[...]
        if bias is not None:
            s = s + bias
        cm = jnp.max(s, axis=0, keepdims=True)
        if far is not None:
            cm = cm + far
        m_old = m_ref[...]
        m_new = jnp.maximum(m_old, cm)
        shift = m_new if far is None else m_new - far
        p = jnp.exp2(s - shift)
        alpha = jnp.exp2(m_old - m_new)
        l_ref[...] = alpha * l_ref[...] + jnp.sum(p, axis=0, keepdims=True)
        acc_ref[...] = alpha * acc_ref[...] + jnp.dot(v_tile(j), p.astype(BF16),
                                                      preferred_element_type=F32)
        m_ref[...] = m_new

    far = far_ref[0]
    scores(i, sa_ref)
    scores(i - 1, sb_ref)
    first_step(i, sa_ref, near_ref[0, 1])

    @pl.when(i >= 1)
    def _():
        scores(i - 2, sa_ref)
        softmax_pv(i - 1, sb_ref, bias=near_ref[0, 0])

    n_far = jnp.maximum(i - 1, 0)

    def far_pair(j):
        scores(j - 1, sb_ref)
        softmax_pv(j, sa_ref, far=far)
        scores(j - 2, sa_ref)
        softmax_pv(j - 1, sb_ref, far=far)

    def far_quad(t, carry):
        far_pair(i - 2 - 4 * t)
        far_pair(i - 4 - 4 * t)
        return carry

    n_quad = n_far // 4
    lax.fori_loop(0, n_quad, far_quad, 0)

    @pl.when(n_far % 4 >= 2)
    def _():
        far_pair(i - 2 - 4 * n_quad)

    @pl.when(n_far % 2 == 1)
    def _():
        softmax_pv(0, sa_ref, far=far)

    lam, out_scale = sc_ref[0], sc_ref[1]
    a = acc_ref[...] * (1.0 / l_ref[...])
    o = a[:, :blk] - lam * a[:, blk:]
    o = o * lax.rsqrt(jnp.mean(o * o, axis=0, keepdims=True) + SUBLN_EPS) * g_ref[...] * out_scale
    o_ref[...] = o.T.astype(BF16)


def _attention(q2, k2, vt, near, far, scalars, subln_g, n_batch, length):
    blk = ATTN_BLOCK
    nq = length // blk
    n = n_batch * length
    return pl.pallas_call(
        _attn_kernel,
        grid=(n_batch, ATTN_HEADS, nq),
        in_specs=[
            pl.BlockSpec(memory_space=pltpu.SMEM),
            pl.BlockSpec((blk, HEAD_LANES), lambda b, h, i: (b * nq + i, h)),
            pl.BlockSpec((length, HEAD_LANES), lambda b, h, i: (b, h)),
            pl.BlockSpec((None, None, length // INPROJ_ROWS, VT_ROWS, INPROJ_ROWS),
                         lambda b, h, i: (b, h, 0, 0, 0)),
            pl.BlockSpec((1, 2, blk, 2 * blk), lambda b, h, i: (h, 0, 0, 0)),
            pl.BlockSpec((1, 1, 2 * blk), lambda b, h, i: (h, 0, 0)),
            pl.BlockSpec((HEAD_LANES, 1), lambda b, h, i: (0, 0)),
        ],
        out_specs=pl.BlockSpec((blk, HEAD_LANES), lambda b, h, i: (b * nq + i, h)),
        out_shape=jax.ShapeDtypeStruct((n, D_V), BF16),
        scratch_shapes=[
            pltpu.VMEM((2 * blk, HEAD_LANES), BF16),
            pltpu.VMEM((blk, 2 * blk), F32),
            pltpu.VMEM((blk, 2 * blk), F32),
            pltpu.VMEM((1, 2 * blk), F32),
            pltpu.VMEM((1, 2 * blk), F32),
            pltpu.VMEM((VT_ROWS, 2 * blk), F32),
        ],
        compiler_params=_params(("parallel", "parallel", "arbitrary")),
        name="diff_attention",
    )(scalars, q2, k2, vt, near, far, subln_g.reshape(HEAD_LANES, 1))


def _merge_kernel(x_ref, ys_ref, cc_ref, cp_ref, ao_ref, gl_ref, wglu_ref, wso_ref, dw_ref, dwb_ref,
                  lng_ref, lnb_ref, wco_ref, wao_ref, wout_ref, pg_ref, o_ref, gbuf_ref, gsh_ref,
                  *, tiles_per_seq):
    tm = MERGE_ROWS
    i = pl.program_id(0)
    z = jnp.dot(ys_ref[...], wglu_ref[...], preferred_element_type=F32)
    glu = (z[:, :D_SSM] * jax.nn.sigmoid(z[:, D_SSM:])).astype(BF16)
    y_a = jnp.dot(glu, wso_ref[...], preferred_element_type=F32)
    m = jax.nn.sigmoid(gl_ref[:, 0:D_MODEL]) * y_a
    cc = cc_ref[...]
    cp = cp_ref[...]
    g_prev = cp[:, :D_CONV] * jax.nn.sigmoid(cp[:, D_CONV:])
    g_prev = jnp.where(i % tiles_per_seq == 0, jnp.zeros_like(g_prev), g_prev)
    gbuf_ref[0:CONV_HALO, :] = g_prev
    gbuf_ref[CONV_HALO:, :] = cc[:, :D_CONV] * jax.nn.sigmoid(cc[:, D_CONV:])
    shifted_rows = tm + CONV_HALO - SUBLANES
    for r in range(1, SUBLANES):
        gsh_ref[r - 1] = gbuf_ref[r:r + shifted_rows, :]
    y = jnp.zeros((tm, D_CONV), F32) + dwb_ref[...]
    base = CONV_HALO - (CONV_WIDTH - 1)
    for tap in range(CONV_WIDTH):
        r, lo = (base + tap) % SUBLANES, (base + tap) // SUBLANES * SUBLANES
        win = gbuf_ref[lo:lo + tm, :] if r == 0 else gsh_ref[r - 1, lo:lo + tm, :]
        y = y + win * dw_ref[tap:tap + 1, :]
    mu = jnp.mean(y, axis=-1, keepdims=True)
    yc = y - mu
    var = jnp.mean(yc * yc, axis=-1, keepdims=True)
    yn = yc * lax.rsqrt(var + LN_EPS) * lng_ref[...] + lnb_ref[...]
    y_b = jnp.dot(jax.nn.silu(yn).astype(BF16), wco_ref[...], preferred_element_type=F32)
    m = m + jax.nn.sigmoid(gl_ref[:, D_MODEL:2 * D_MODEL]) * y_b
    y_c = jnp.dot(ao_ref[...], wao_ref[...], preferred_element_type=F32)
    m = m + jax.nn.sigmoid(gl_ref[:, 2 * D_MODEL:3 * D_MODEL]) * y_c
    r = jnp.dot(m.astype(BF16), wout_ref[...], preferred_element_type=F32)
    o_ref[...] = x_ref[...] + _rms(r, pg_ref[...])


def _merge(x2, ys, c_in, ao, gl, w_glu, w_so, dw, dw_b, ln_g, ln_b, w_co, w_ao, w_out, post_g, length):
    n = x2.shape[0]
    tm = MERGE_ROWS
    halo_blocks = tm // CONV_HALO
    row = lambda width: pl.BlockSpec((tm, width), lambda i: (i, 0))
    dw_pad = jnp.zeros((CONV_HALO, D_CONV), F32).at[:CONV_WIDTH].set(dw)
    return pl.pallas_call(
        functools.partial(_merge_kernel, tiles_per_seq=length // tm),
        grid=(n // tm,),
        in_specs=[
            row(D_MODEL), row(D_SSM), row(2 * D_CONV),
            pl.BlockSpec((CONV_HALO, 2 * D_CONV), lambda i: (jnp.maximum(i * halo_blocks - 1, 0), 0)),
            row(D_V), row(N_BRANCH * D_MODEL),
            _resident((D_SSM, 2 * D_SSM)), _resident((D_SSM, D_MODEL)),
            _resident((CONV_HALO, D_CONV)), _resident((1, D_CONV)), _resident((1, D_CONV)),
            _resident((1, D_CONV)), _resident((D_CONV, D_MODEL)), _resident((D_V, D_MODEL)),
            _resident((D_MODEL, D_MODEL)), _resident((1, D_MODEL)),
        ],
        out_specs=row(D_MODEL),
        out_shape=jax.ShapeDtypeStruct((n, D_MODEL), F32),
        scratch_shapes=[pltpu.VMEM((CONV_HALO + tm, D_CONV), F32),
                        pltpu.VMEM((SUBLANES - 1, CONV_HALO + tm - SUBLANES, D_CONV), F32)],
        compiler_params=_params(("parallel",)),
        name="merge",
    )(x2, ys, c_in, c_in, ao, gl, w_glu, w_so, dw_pad, dw_b, ln_g, ln_b, w_co, w_ao, w_out, post_g)


def _ffn_kernel(x_ref, g1_ref, win_ref, wout_ref, g2_ref, o_ref):
    x = x_ref[...]
    hb = _rms(x, g1_ref[...]).astype(BF16)
    f = jnp.zeros(x.shape, F32)
    for c in range(D_FF // FFN_CHUNK):
        lo = c * FFN_CHUNK
        gate = jnp.dot(hb, win_ref[:, lo:lo + FFN_CHUNK], preferred_element_type=F32)
        up = jnp.dot(hb, win_ref[:, D_FF + lo:D_FF + lo + FFN_CHUNK], preferred_element_type=F32)
        act = (jax.nn.silu(gate) * up).astype(BF16)
        f = f + jnp.dot(act, wout_ref[lo:lo + FFN_CHUNK, :], preferred_element_type=F32)
    o_ref[...] = x + _rms(f, g2_ref[...])


def _ffn(x2, pre_g, w_in, w_out, post_g):
    n = x2.shape[0]
    tm = FFN_ROWS
    row = pl.BlockSpec((tm, D_MODEL), lambda i: (i, 0))
    return pl.pallas_call(
        _ffn_kernel,
        grid=(n // tm,),
        in_specs=[row, _resident((1, D_MODEL)), _resident((D_MODEL, 2 * D_FF)),
                  _resident((D_FF, D_MODEL)), _resident((1, D_MODEL))],
        out_specs=row,
        out_shape=jax.ShapeDtypeStruct((n, D_MODEL), F32),
        compiler_params=_params(("parallel",)),
        name="ffn",
    )(x2, pre_g, w_in, w_out, post_g)


def kernel(x, rel_bias, pre_mix_g, w_in, ssm_a_re, ssm_a_im, ssm_log_dt, ssm_b_re, ssm_b_im, ssm_c_re, ssm_c_im, ssm_d, w_ssm_glu, w_ssm_out, conv_dw, conv_dw_b, conv_ln_g, conv_ln_b, w_conv_out, lambda_q1, lambda_k1, lambda_q2, lambda_k2, attn_subln_g, w_attn_out, w_out, post_mix_g, pre_ffn_g, w_ffn_in, w_ffn_out, post_ffn_g):
    n_batch, length, _ = x.shape
    depth = w_in.shape[0]
    assert length % max(ATTN_BLOCK, MERGE_ROWS, FFN_ROWS, INPROJ_ROWS, SSM_CHUNK) == 0
    x2 = x.astype(F32).reshape(n_batch * length, D_MODEL)
    near, far = _attn_bias(rel_bias)
    vec = lambda a: a.astype(F32).reshape(1, -1)
    w_in_b = w_in.astype(BF16)
    w_vt = jnp.swapaxes(w_in[:, :, OFF_V:OFF_GATE], 1, 2).astype(BF16)
    for layer in range(depth):
        lambda_init = 0.8 - 0.6 * math.exp(-0.3 * layer)
        lam = (jnp.exp(jnp.sum(lambda_q1[layer].astype(F32) * lambda_k1[layer].astype(F32)))
               - jnp.exp(jnp.sum(lambda_q2[layer].astype(F32) * lambda_k2[layer].astype(F32)))
               + lambda_init)
        scalars = jnp.stack([lam, jnp.asarray(1.0 - lambda_init, F32)]).astype(F32)
        u, c_in, q, k, vt, gl = _inproj(x2, vec(pre_mix_g[layer]), w_in_b[layer], w_vt[layer], n_batch, length)
        mats = _ssm_matrices(ssm_a_re[layer].astype(F32), ssm_a_im[layer].astype(F32),
                             ssm_log_dt[layer].astype(F32), ssm_b_re[layer].astype(F32),
                             ssm_b_im[layer].astype(F32), ssm_c_re[layer].astype(F32),
                             ssm_c_im[layer].astype(F32))
        ys = _ssm(u, mats, ssm_d[layer].astype(F32), n_batch, length)
        ao = _attention(q, k, vt, near, far, scalars, attn_subln_g[layer].astype(F32), n_batch, length)
        x2 = _merge(x2, ys, c_in, ao, gl, w_ssm_glu[layer].astype(BF16), w_ssm_out[layer].astype(BF16),
                    conv_dw[layer].astype(F32), vec(conv_dw_b[layer]), vec(conv_ln_g[layer]),
                    vec(conv_ln_b[layer]), w_conv_out[layer].astype(BF16), w_attn_out[layer].astype(BF16),
                    w_out[layer].astype(BF16), vec(post_mix_g[layer]), length)
        x2 = _ffn(x2, vec(pre_ffn_g[layer]), w_ffn_in[layer].astype(BF16), w_ffn_out[layer].astype(BF16),
                  vec(post_ffn_g[layer]))
    return x2.reshape(n_batch, length, D_MODEL).astype(x.dtype)
```

```python
import functools
import math

import jax
import jax.numpy as jnp
from jax import lax
from jax.experimental import pallas as pl
from jax.experimental.pallas import tpu as pltpu

F32 = jnp.float32
BF16 = jnp.bfloat16

D_MODEL = 1024
SSM_GROUP = 16
D_SSM = D_MODEL // 2
SSM_GROUPS = D_SSM // SSM_GROUP
SSM_STATE = 64
D_CONV = D_MODEL // 2
CONV_WIDTH = 31
ATTN_HEADS = 8
ATTN_HEAD_DIM = 64
ATTN_MAPS = 2 * ATTN_HEADS
D_QK = ATTN_MAPS * ATTN_HEAD_DIM
D_V = ATTN_HEADS * 2 * ATTN_HEAD_DIM
REL_BUCKETS = 32
REL_MAX_DIST = 128
D_FF = 2816
N_BRANCH = 3
OFF_SSM = 0
OFF_CONV = OFF_SSM + D_SSM
OFF_Q = OFF_CONV + 2 * D_CONV
OFF_K = OFF_Q + D_QK
OFF_V = OFF_K + D_QK
OFF_GATE = OFF_V + D_V
D_IN = OFF_GATE + N_BRANCH * D_MODEL
RMS_EPS = 1e-6
SUBLN_EPS = 1e-5
LN_EPS = 1e-5
LOG2_E = math.log2(math.e)

V7X_VMEM_LIMIT_BYTES = 56 * 1024 * 1024
SUBLANES = 8
HEAD_LANES = 2 * ATTN_HEAD_DIM

SSM_CHUNK = 32
CONV_HALO = 32
ATTN_BLOCK = 512
VT_ROWS = HEAD_LANES
INPROJ_ROWS = 256
MERGE_ROWS = 256
FFN_ROWS = 512
FFN_CHUNK = 256


def _params(sem):
    return pltpu.CompilerParams(dimension_semantics=sem, vmem_limit_bytes=V7X_VMEM_LIMIT_BYTES)


def _resident(shape):
    nd = len(shape)
    return pl.BlockSpec(shape, lambda *_: (0,) * nd, pipeline_mode=pl.Buffered(1))


def _rms(x, g):
    return x * lax.rsqrt(jnp.mean(x * x, axis=-1, keepdims=True) + RMS_EPS) * g


def _inproj_kernel(x_ref, g_ref, w_ref, wvt_ref, u_ref, c_ref, q_ref, k_ref, vt_ref, gl_ref):
    hb = _rms(x_ref[...], g_ref[...]).astype(BF16)

    def proj(lo, hi):
        return jnp.dot(hb, w_ref[:, lo:hi], preferred_element_type=F32)

    u_ref[...] = proj(OFF_SSM, OFF_CONV).astype(BF16)
    c_ref[...] = proj(OFF_CONV, OFF_Q)
    q_ref[...] = (proj(OFF_Q, OFF_K) * (LOG2_E * ATTN_HEAD_DIM ** -0.5)).astype(BF16)
    k_ref[...] = proj(OFF_K, OFF_V).astype(BF16)
    vt = lax.dot_general(wvt_ref[...], hb, (((1,), (1,)), ((), ())), preferred_element_type=F32)
    vt_ref[...] = vt.astype(BF16).reshape(ATTN_HEADS, VT_ROWS, INPROJ_ROWS)
    for b in range(N_BRANCH):
        lo = OFF_GATE + b * D_MODEL
        gl_ref[:, b * D_MODEL:(b + 1) * D_MODEL] = proj(lo, lo + D_MODEL)


def _inproj(x2, g, w, wvt, n_batch, length):
    n = x2.shape[0]
    tm = INPROJ_ROWS
    tiles = length // tm
    row = lambda width: pl.BlockSpec((tm, width), lambda i: (i, 0))
    return pl.pallas_call(
        _inproj_kernel,
        grid=(n // tm,),
        in_specs=[row(D_MODEL), _resident((1, D_MODEL)), _resident((D_MODEL, D_IN)),
                  _resident((D_V, D_MODEL))],
        out_specs=[row(D_SSM), row(2 * D_CONV), row(D_QK), row(D_QK),
                   pl.BlockSpec((None, ATTN_HEADS, None, VT_ROWS, tm),
                                lambda i: (i // tiles, 0, i % tiles, 0, 0)),
                   row(N_BRANCH * D_MODEL)],
        out_shape=[
            jax.ShapeDtypeStruct((n, D_SSM), BF16),
            jax.ShapeDtypeStruct((n, 2 * D_CONV), F32),
            jax.ShapeDtypeStruct((n, D_QK), BF16),
            jax.ShapeDtypeStruct((n, D_QK), BF16),
            jax.ShapeDtypeStruct((n_batch, ATTN_HEADS, tiles, VT_ROWS, tm), BF16),
            jax.ShapeDtypeStruct((n, N_BRANCH * D_MODEL), F32),
        ],
        compiler_params=_params(("parallel",)),
        name="inproj",
    )(x2, g, w, wvt)


def _ssm_matrices(a_re, a_im, log_dt, b_re, b_im, c_re, c_im):
    hp = lax.Precision.HIGHEST
    t_len = SSM_CHUNK
    dt = jnp.exp(log_dt)[:, None]
    zr, zi = a_re * dt, a_im * dt
    er = jnp.exp(zr)
    lbr, lbi = er * jnp.cos(zi), er * jnp.sin(zi)
    den = a_re * a_re + a_im * a_im
    nr, ni = lbr - 1.0, lbi
    cr, ci = (nr * a_re + ni * a_im) / den, (ni * a_re - nr * a_im) / den
    bbr = cr[..., None] * b_re - ci[..., None] * b_im
    bbi = cr[..., None] * b_im + ci[..., None] * b_re
    tau = jnp.arange(t_len + 1, dtype=F32)[:, None, None]
    pe = jnp.exp(tau * zr)
    pwr, pwi = pe * jnp.cos(tau * zi), pe * jnp.sin(tau * zi)
    e_r = pwr[:t_len, :, :, None] * bbr - pwi[:t_len, :, :, None] * bbi
    e_i = pwr[:t_len, :, :, None] * bbi + pwi[:t_len, :, :, None] * bbr
    kern = (jnp.einsum("gcp,tgpd->tgcd", c_re, e_r, precision=hp)
            - jnp.einsum("gcp,tgpd->tgcd", c_im, e_i, precision=hp))
    lag = jnp.arange(t_len)[None, :] - jnp.arange(t_len)[:, None]
    pick = (lag[:, :, None] == jnp.arange(t_len)[None, None, :]).astype(BF16)
    gc = SSM_GROUP * t_len
    w_intra = jnp.einsum("stu,ugcd->gdsct", pick, kern.astype(BF16),
                         preferred_element_type=F32).astype(BF16).reshape(SSM_GROUPS, gc, gc)
    w_sr = e_r[::-1].transpose(1, 3, 0, 2).reshape(SSM_GROUPS, gc, SSM_STATE)
    w_si = e_i[::-1].transpose(1, 3, 0, 2).reshape(SSM_GROUPS, gc, SSM_STATE)
    p1r, p1i = pwr[1:, :, None, :], pwi[1:, :, None, :]
    w_or = (c_re * p1r - c_im * p1i).transpose(1, 3, 2, 0).reshape(SSM_GROUPS, SSM_STATE, gc)
    w_oi = (-(c_re * p1i + c_im * p1r)).transpose(1, 3, 2, 0).reshape(SSM_GROUPS, SSM_STATE, gc)
    a_r = pwr[t_len][:, None, :]
    a_i = pwi[t_len][:, None, :]
    return (w_intra, w_sr.astype(BF16), w_si.astype(BF16),
            w_or.astype(BF16), w_oi.astype(BF16), a_r, a_i)


def _ssm_kernel(u_ref, wi_ref, wsr_ref, wsi_ref, wor_ref, woi_ref, ar_ref, ai_ref, d_ref,
                o_ref, sr_ref, si_ref, xr_ref, xi_ref, *, n_batch, n_chunk):
    ub = u_ref[...]
    sr_ref[...] = jnp.dot(ub, wsr_ref[0], preferred_element_type=F32)
    si_ref[...] = jnp.dot(ub, wsi_ref[0], preferred_element_type=F32)
    a_r, a_i = ar_ref[0], ai_ref[0]

    def step(c, carry):
        nxt = []
        for b in range(n_batch):
            x_r, x_i = carry[2 * b], carry[2 * b + 1]
            row = pl.ds(b * n_chunk + c, 1)
            xr_ref[row, :] = x_r
            xi_ref[row, :] = x_i
            nxt.append(a_r * x_r - a_i * x_i + sr_ref[row, :])
            nxt.append(a_r * x_i + a_i * x_r + si_ref[row, :])
        return tuple(nxt)

    zero = jnp.zeros((1, SSM_STATE), F32)
    lax.fori_loop(0, n_chunk, step, (zero,) * (2 * n_batch))
    y = (jnp.dot(ub, wi_ref[0], preferred_element_type=F32)
         + jnp.dot(xr_ref[...].astype(BF16), wor_ref[0], preferred_element_type=F32)
         + jnp.dot(xi_ref[...].astype(BF16), woi_ref[0], preferred_element_type=F32)
         + d_ref[0] * ub.astype(F32))
    o_ref[...] = jax.nn.gelu(y, approximate=True).astype(BF16)


def _ssm(u2, mats, d_skip, n_batch, length):
    t_len = SSM_CHUNK
    n_chunk = length // t_len
    rows = n_batch * n_chunk
    gc = SSM_GROUP * t_len
    ug = u2.reshape(rows, t_len, D_SSM).transpose(0, 2, 1).reshape(rows, D_SSM * t_len)
    d_t = jnp.repeat(d_skip, t_len, axis=1).reshape(SSM_GROUPS, 1, gc)
    w_intra, w_sr, w_si, w_or, w_oi, a_r, a_i = mats
    grp = lambda *shape: pl.BlockSpec((1,) + shape, lambda g: (g, 0, 0))
    cols = pl.BlockSpec((rows, gc), lambda g: (0, g))
    yg = pl.pallas_call(
        functools.partial(_ssm_kernel, n_batch=n_batch, n_chunk=n_chunk),
        grid=(SSM_GROUPS,),
        in_specs=[cols, grp(gc, gc), grp(gc, SSM_STATE), grp(gc, SSM_STATE),
                  grp(SSM_STATE, gc), grp(SSM_STATE, gc), grp(1, SSM_STATE), grp(1, SSM_STATE),
                  grp(1, gc)],
        out_specs=cols,
        out_shape=jax.ShapeDtypeStruct((rows, D_SSM * t_len), BF16),
        scratch_shapes=[pltpu.VMEM((rows, SSM_STATE), F32)] * 4,
        compiler_params=_params(("parallel",)),
        name="ssm",
    )(ug, w_intra, w_sr, w_si, w_or, w_oi, a_r, a_i, d_t)
    return yg.reshape(rows, D_SSM, t_len).transpose(0, 2, 1).reshape(n_batch * length, D_SSM)


def _rel_bucket(rel):
    n = jnp.maximum(rel, 0)
    max_exact = REL_BUCKETS // 2
    nf = jnp.maximum(n, max_exact).astype(F32)
    large = max_exact + (jnp.log(nf / max_exact) / math.log(REL_MAX_DIST / max_exact)
                         * (REL_BUCKETS - max_exact)).astype(jnp.int32)
    large = jnp.minimum(large, REL_BUCKETS - 1)
    return jnp.where(n < max_exact, n, large)


def _attn_bias(rel_bias):
    blk = ATTN_BLOCK
    assert blk + 1 >= REL_MAX_DIST
    rb = rel_bias.astype(F32) * LOG2_E
    rel_diag = jnp.arange(blk)[None, :] - jnp.arange(blk)[:, None]
    rel = jnp.stack([rel_diag + blk, rel_diag])
    pick = (_rel_bucket(rel)[..., None] == jnp.arange(REL_BUCKETS)).astype(F32)
    near = jnp.einsum("kjib,bhn->hkjni", pick, rb.reshape(REL_BUCKETS, ATTN_HEADS, 2),
                      precision=lax.Precision.HIGHEST)
    near = jnp.where((rel >= 0)[None, :, :, None, :], near, -jnp.inf)
    near = near.reshape(ATTN_HEADS, 2, blk, 2 * blk)
    far = rb[_rel_bucket(jnp.full((), blk + 1, jnp.int32))]
    far = jnp.broadcast_to(far.reshape(ATTN_HEADS, 2, 1), (ATTN_HEADS, 2, blk))
    return near, far.reshape(ATTN_HEADS, 1, 2 * blk)


def _attn_kernel(sc_ref, q_ref, k_ref, vt_ref, near_ref, far_ref, g_ref, o_ref,
                 qs_ref, sa_ref, sb_ref, m_ref, l_ref, acc_ref):
    blk = ATTN_BLOCK
    n_sub = blk // INPROJ_ROWS
    i = pl.program_id(2)
    q = q_ref[...]
    lane = lax.broadcasted_iota(jnp.int32, q.shape, 1)
    zero = jnp.zeros_like(q)
    qs_ref[0:blk, :] = jnp.where(lane < ATTN_HEAD_DIM, q, zero)
    qs_ref[blk:, :] = jnp.where(lane >= ATTN_HEAD_DIM, q, zero)

    def scores(j, dst_ref):
        j = jnp.maximum(j, 0)
        kb = k_ref[pl.ds(pl.multiple_of(j * blk, blk), blk), :]
        dst_ref[...] = lax.dot_general(kb, qs_ref[...], (((1,), (1,)), ((), ())),
                                       preferred_element_type=F32)

    def v_tile(j):
        return jnp.concatenate([vt_ref[j * n_sub + r] for r in range(n_sub)], axis=1)

    def first_step(j, src_ref, bias):
        s = src_ref[...] + bias
        m_new = jnp.max(s, axis=0, keepdims=True)
        p = jnp.exp2(s - m_new)
        l_ref[...] = jnp.sum(p, axis=0, keepdims=True)
        acc_ref[...] = jnp.dot(v_tile(j), p.astype(BF16), preferred_element_type=F32)
        m_ref[...] = m_new

    def softmax_pv(j, src_ref, bias=None, far=None):
        s = src_ref[...]
        if bias is not None:
            s = s + bias
        cm = jnp.max(s, axis=0, keepdims=True)
        if far is not None:
            cm = cm + far
        m_old = m_ref[...]
        m_new = jnp.maximum(m_old, cm)
        shift = m_new if far is None else m_new - far
        p = jnp.exp2(s - shift)
        alpha = jnp.exp2(m_old - m_new)
        l_ref[...] = alpha * l_ref[...] + jnp.sum(p, axis=0, keepdims=True)
        acc_ref[...] = alpha * acc_ref[...] + jnp.dot(v_tile(j), p.astype(BF16),
                                                      preferred_element_type=F32)
        m_ref[...] = m_new

    far = far_ref[0]
    scores(i, sa_ref)
    scores(i - 1, sb_ref)
    first_step(i, sa_ref, near_ref[0, 1])

    @pl.when(i >= 1)
    def _():
        scores(i - 2, sa_ref)
        softmax_pv(i - 1, sb_ref, bias=near_ref[0, 0])

    n_far = jnp.maximum(i - 1, 0)

    def far_pair(j):
        scores(j - 1, sb_ref)
        softmax_pv(j, sa_ref, far=far)
        scores(j - 2, sa_ref)
        softmax_pv(j - 1, sb_ref, far=far)

    def far_quad(t, carry):
        far_pair(i - 2 - 4 * t)
        far_pair(i - 4 - 4 * t)
        return carry

    n_quad = n_far // 4
    lax.fori_loop(0, n_quad, far_quad, 0)

    @pl.when(n_far % 4 >= 2)
    def _():
        far_pair(i - 2 - 4 * n_quad)

    @pl.when(n_far % 2 == 1)
    def _():
        softmax_pv(0, sa_ref, far=far)

    lam, out_scale = sc_ref[0], sc_ref[1]
    a = acc_ref[...] * (1.0 / l_ref[...])
    o = a[:, :blk] - lam * a[:, blk:]
    o = o * lax.rsqrt(jnp.mean(o * o, axis=0, keepdims=True) + SUBLN_EPS) * g_ref[...] * out_scale
    o_ref[...] = o.T.astype(BF16)


def _attention(q2, k2, vt, near, far, scalars, subln_g, n_batch, length):
    blk = ATTN_BLOCK
    nq = length // blk
    n = n_batch * length
    return pl.pallas_call(
        _attn_kernel,
        grid=(n_batch, ATTN_HEADS, nq),
        in_specs=[
            pl.BlockSpec(memory_space=pltpu.SMEM),
            pl.BlockSpec((blk, HEAD_LANES), lambda b, h, i: (b * nq + i, h)),
            pl.BlockSpec((length, HEAD_LANES), lambda b, h, i: (b, h)),
            pl.BlockSpec((None, None, length // INPROJ_ROWS, VT_ROWS, INPROJ_ROWS),
                         lambda b, h, i: (b, h, 0, 0, 0)),
            pl.BlockSpec((1, 2, blk, 2 * blk), lambda b, h, i: (h, 0, 0, 0)),
            pl.BlockSpec((1, 1, 2 * blk), lambda b, h, i: (h, 0, 0)),
            pl.BlockSpec((HEAD_LANES, 1), lambda b, h, i: (0, 0)),
        ],
        out_specs=pl.BlockSpec((blk, HEAD_LANES), lambda b, h, i: (b * nq + i, h)),
        out_shape=jax.ShapeDtypeStruct((n, D_V), BF16),
        scratch_shapes=[
            pltpu.VMEM((2 * blk, HEAD_LANES), BF16),
            pltpu.VMEM((blk, 2 * blk), F32),
            pltpu.VMEM((blk, 2 * blk), F32),
            pltpu.VMEM((1, 2 * blk), F32),
            pltpu.VMEM((1, 2 * blk), F32),
            pltpu.VMEM((VT_ROWS, 2 * blk), F32),
        ],
        compiler_params=_params(("parallel", "parallel", "arbitrary")),
        name="diff_attention",
    )(scalars, q2, k2, vt, near, far, subln_g.reshape(HEAD_LANES, 1))


def _merge_kernel(x_ref, ys_ref, cc_ref, cp_ref, ao_ref, gl_ref, wglu_ref, wso_ref, dw_ref, dwb_ref,
                  lng_ref, lnb_ref, wco_ref, wao_ref, wout_ref, pg_ref, o_ref, gbuf_ref, gsh_ref,
                  *, tiles_per_seq):
    tm = MERGE_ROWS
    i = pl.program_id(0)
    z = jnp.dot(ys_ref[...], wglu_ref[...], preferred_element_type=F32)
    glu = (z[:, :D_SSM] * jax.nn.sigmoid(z[:, D_SSM:])).astype(BF16)
    y_a = jnp.dot(glu, wso_ref[...], preferred_element_type=F32)
    m = jax.nn.sigmoid(gl_ref[:, 0:D_MODEL]) * y_a
    cc = cc_ref[...]
    cp = cp_ref[...]
    g_prev = cp[:, :D_CONV] * jax.nn.sigmoid(cp[:, D_CONV:])
    g_prev = jnp.where(i % tiles_per_seq == 0, jnp.zeros_like(g_prev), g_prev)
    gbuf_ref[0:CONV_HALO, :] = g_prev
    gbuf_ref[CONV_HALO:, :] = cc[:, :D_CONV] * jax.nn.sigmoid(cc[:, D_CONV:])
    shifted_rows = tm + CONV_HALO - SUBLANES
    for r in range(1, SUBLANES):
        gsh_ref[r - 1] = gbuf_ref[r:r + shifted_rows, :]
    y = jnp.zeros((tm, D_CONV), F32) + dwb_ref[...]
    base = CONV_HALO - (CONV_WIDTH - 1)
    for tap in range(CONV_WIDTH):
        r, lo = (base + tap) % SUBLANES, (base + tap) // SUBLANES * SUBLANES
        win = gbuf_ref[lo:lo + tm, :] if r == 0 else gsh_ref[r - 1, lo:lo + tm, :]
        y = y + win * dw_ref[tap:tap + 1, :]
    mu = jnp.mean(y, axis=-1, keepdims=True)
    yc = y - mu
    var = jnp.mean(yc * yc, axis=-1, keepdims=True)
    yn = yc * lax.rsqrt(var + LN_EPS) * lng_ref[...] + lnb_ref[...]
    y_b = jnp.dot(jax.nn.silu(yn).astype(BF16), wco_ref[...], preferred_element_type=F32)
    m = m + jax.nn.sigmoid(gl_ref[:, D_MODEL:2 * D_MODEL]) * y_b
    y_c = jnp.dot(ao_ref[...], wao_ref[...], preferred_element_type=F32)
    m = m + jax.nn.sigmoid(gl_ref[:, 2 * D_MODEL:3 * D_MODEL]) * y_c
    r = jnp.dot(m.astype(BF16), wout_ref[...], preferred_element_type=F32)
    o_ref[...] = x_ref[...] + _rms(r, pg_ref[...])


def _merge(x2, ys, c_in, ao, gl, w_glu, w_so, dw, dw_b, ln_g, ln_b, w_co, w_ao, w_out, post_g, length):
    n = x2.shape[0]
    tm = MERGE_ROWS
    halo_blocks = tm // CONV_HALO
    row = lambda width: pl.BlockSpec((tm, width), lambda i: (i, 0))
    dw_pad = jnp.zeros((CONV_HALO, D_CONV), F32).at[:CONV_WIDTH].set(dw)
    return pl.pallas_call(
        functools.partial(_merge_kernel, tiles_per_seq=length // tm),
        grid=(n // tm,),
        in_specs=[
            row(D_MODEL), row(D_SSM), row(2 * D_CONV),
            pl.BlockSpec((CONV_HALO, 2 * D_CONV), lambda i: (jnp.maximum(i * halo_blocks - 1, 0), 0)),
            row(D_V), row(N_BRANCH * D_MODEL),
            _resident((D_SSM, 2 * D_SSM)), _resident((D_SSM, D_MODEL)),
            _resident((CONV_HALO, D_CONV)), _resident((1, D_CONV)), _resident((1, D_CONV)),
            _resident((1, D_CONV)), _resident((D_CONV, D_MODEL)), _resident((D_V, D_MODEL)),
            _resident((D_MODEL, D_MODEL)), _resident((1, D_MODEL)),
        ],
        out_specs=row(D_MODEL),
        out_shape=jax.ShapeDtypeStruct((n, D_MODEL), F32),
        scratch_shapes=[pltpu.VMEM((CONV_HALO + tm, D_CONV), F32),
                        pltpu.VMEM((SUBLANES - 1, CONV_HALO + tm - SUBLANES, D_CONV), F32)],
        compiler_params=_params(("parallel",)),
        name="merge",
    )(x2, ys, c_in, c_in, ao, gl, w_glu, w_so, dw_pad, dw_b, ln_g, ln_b, w_co, w_ao, w_out, post_g)


def _ffn_kernel(x_ref, g1_ref, win_ref, wout_ref, g2_ref, o_ref):
    x = x_ref[...]
    hb = _rms(x, g1_ref[...]).astype(BF16)
    f = jnp.zeros(x.shape, F32)
    for c in range(D_FF // FFN_CHUNK):
        lo = c * FFN_CHUNK
        gate = jnp.dot(hb, win_ref[:, lo:lo + FFN_CHUNK], preferred_element_type=F32)
        up = jnp.dot(hb, win_ref[:, D_FF + lo:D_FF + lo + FFN_CHUNK], preferred_element_type=F32)
        act = (jax.nn.silu(gate) * up).astype(BF16)
        f = f + jnp.dot(act, wout_ref[lo:lo + FFN_CHUNK, :], preferred_element_type=F32)
    o_ref[...] = x + _rms(f, g2_ref[...])


def _ffn(x2, pre_g, w_in, w_out, post_g):
    n = x2.shape[0]
    tm = FFN_ROWS
    row = pl.BlockSpec((tm, D_MODEL), lambda i: (i, 0))
    return pl.pallas_call(
        _ffn_kernel,
        grid=(n // tm,),
        in_specs=[row, _resident((1, D_MODEL)), _resident((D_MODEL, 2 * D_FF)),
                  _resident((D_FF, D_MODEL)), _resident((1, D_MODEL))],
        out_specs=row,
        out_shape=jax.ShapeDtypeStruct((n, D_MODEL), F32),
        compiler_params=_params(("parallel",)),
        name="ffn",
    )(x2, pre_g, w_in, w_out, post_g)


def kernel(x, rel_bias, pre_mix_g, w_in, ssm_a_re, ssm_a_im, ssm_log_dt, ssm_b_re, ssm_b_im, ssm_c_re, ssm_c_im, ssm_d, w_ssm_glu, w_ssm_out, conv_dw, conv_dw_b, conv_ln_g, conv_ln_b, w_conv_out, lambda_q1, lambda_k1, lambda_q2, lambda_k2, attn_subln_g, w_attn_out, w_out, post_mix_g, pre_ffn_g, w_ffn_in, w_ffn_out, post_ffn_g):
    n_batch, length, _ = x.shape
    depth = w_in.shape[0]
    assert length % max(ATTN_BLOCK, MERGE_ROWS, FFN_ROWS, INPROJ_ROWS, SSM_CHUNK) == 0
    x2 = x.astype(F32).reshape(n_batch * length, D_MODEL)
    near, far = _attn_bias(rel_bias)
    vec = lambda a: a.astype(F32).reshape(1, -1)
    w_in_b = w_in.astype(BF16)
    w_vt = jnp.swapaxes(w_in[:, :, OFF_V:OFF_GATE], 1, 2).astype(BF16)
    for layer in range(depth):
        lambda_init = 0.8 - 0.6 * math.exp(-0.3 * layer)
        lam = (jnp.exp(jnp.sum(lambda_q1[layer].astype(F32) * lambda_k1[layer].astype(F32)))
               - jnp.exp(jnp.sum(lambda_q2[layer].astype(F32) * lambda_k2[layer].astype(F32)))
               + lambda_init)
        scalars = jnp.stack([lam, jnp.asarray(1.0 - lambda_init, F32)]).astype(F32)
        u, c_in, q, k, vt, gl = _inproj(x2, vec(pre_mix_g[layer]), w_in_b[layer], w_vt[layer], n_batch, length)
        mats = _ssm_matrices(ssm_a_re[layer].astype(F32), ssm_a_im[layer].astype(F32),
                             ssm_log_dt[layer].astype(F32), ssm_b_re[layer].astype(F32),
                             ssm_b_im[layer].astype(F32), ssm_c_re[layer].astype(F32),
                             ssm_c_im[layer].astype(F32))
        ys = _ssm(u, mats, ssm_d[layer].astype(F32), n_batch, length)
        ao = _attention(q, k, vt, near, far, scalars, attn_subln_g[layer].astype(F32), n_batch, length)
        x2 = _merge(x2, ys, c_in, ao, gl, w_ssm_glu[layer].astype(BF16), w_ssm_out[layer].astype(BF16),
                    conv_dw[layer].astype(F32), vec(conv_dw_b[layer]), vec(conv_ln_g[layer]),
                    vec(conv_ln_b[layer]), w_conv_out[layer].astype(BF16), w_attn_out[layer].astype(BF16),
                    w_out[layer].astype(BF16), vec(post_mix_g[layer]), length)
        x2 = _ffn(x2, vec(pre_ffn_g[layer]), w_ffn_in[layer].astype(BF16), w_ffn_out[layer].astype(BF16),
                  vec(post_ffn_g[layer]))
    return x2.reshape(n_batch, length, D_MODEL).astype(x.dtype)
```

```python
import functools
import math

import jax
import jax.numpy as jnp
from jax import lax
from jax.experimental import pallas as pl
from jax.experimental.pallas import tpu as pltpu

F32 = jnp.float32
BF16 = jnp.bfloat16

D_MODEL = 1024
SSM_GROUP = 16
D_SSM = D_MODEL // 2
SSM_GROUPS = D_SSM // SSM_GROUP
SSM_STATE = 64
D_CONV = D_MODEL // 2
CONV_WIDTH = 31
ATTN_HEADS = 8
ATTN_HEAD_DIM = 64
ATTN_MAPS = 2 * ATTN_HEADS
D_QK = ATTN_MAPS * ATTN_HEAD_DIM
D_V = ATTN_HEADS * 2 * ATTN_HEAD_DIM
REL_BUCKETS = 32
REL_MAX_DIST = 128
D_FF = 2816
N_BRANCH = 3
OFF_SSM = 0
OFF_CONV = OFF_SSM + D_SSM
OFF_Q = OFF_CONV + 2 * D_CONV
OFF_K = OFF_Q + D_QK
OFF_V = OFF_K + D_QK
OFF_GATE = OFF_V + D_V
D_IN = OFF_GATE + N_BRANCH * D_MODEL
RMS_EPS = 1e-6
SUBLN_EPS = 1e-5
LN_EPS = 1e-5
LOG2_E = math.log2(math.e)

V7X_VMEM_LIMIT_BYTES = 56 * 1024 * 1024
SUBLANES = 8
HEAD_LANES = 2 * ATTN_HEAD_DIM

SSM_CHUNK = 32
CONV_HALO = 32
PROJ_COLS = 256
CONV_SPREAD = 20
ATTN_BLOCK = 512
ATTN_HEAD_GROUP = 2
VT_ROWS = HEAD_LANES
INPROJ_ROWS = 256
MERGE_ROWS = 256
FFN_ROWS = 512
FFN_CHUNK = 256


def _params(sem):
    return pltpu.CompilerParams(dimension_semantics=sem, vmem_limit_bytes=V7X_VMEM_LIMIT_BYTES)


def _resident(shape):
    nd = len(shape)
    return pl.BlockSpec(shape, lambda *_: (0,) * nd, pipeline_mode=pl.Buffered(1))


def _rms(x, g):
    return x * lax.rsqrt(jnp.mean(x * x, axis=-1, keepdims=True) + RMS_EPS) * g


def _inproj_kernel(x_ref, g_ref, w_ref, wvt_ref, dw_ref, dwb_ref, lng_ref, lnb_ref,
                   u_ref, yb_ref, q_ref, k_ref, vt_ref, gl_ref, halo_ref, gbuf_ref, gsh_ref, yacc_ref,
                   *, tiles_per_seq):
    tm = INPROJ_ROWS
    hb = _rms(x_ref[...], g_ref[...]).astype(BF16)

    def proj(lo, hi):
        return jnp.dot(hb, w_ref[:, lo:hi], preferred_element_type=F32)

    u_ref[...] = proj(OFF_SSM, OFF_CONV).astype(BF16)

    @pl.when(pl.program_id(0) % tiles_per_seq == 0)
    def _():
        halo_ref[...] = jnp.zeros(halo_ref.shape, F32)

    c = proj(OFF_CONV, OFF_Q)
    gbuf_ref[0:CONV_HALO, :] = halo_ref[...]
    gbuf_ref[CONV_HALO:, :] = c[:, :D_CONV] * jax.nn.sigmoid(c[:, D_CONV:])
    halo_ref[...] = gbuf_ref[tm:, :]
    shifted_rows = tm + CONV_HALO - SUBLANES
    for r in range(1, SUBLANES):
        gsh_ref[r - 1] = gbuf_ref[r:r + shifted_rows, :]

    def conv_taps(taps, y):
        base = CONV_HALO - (CONV_WIDTH - 1)
        for tap in taps:
            r, lo = (base + tap) % SUBLANES, (base + tap) // SUBLANES * SUBLANES
            win = gbuf_ref[lo:lo + tm, :] if r == 0 else gsh_ref[r - 1, lo:lo + tm, :]
            y = y + win * dw_ref[tap:tap + 1, :]
        return y

    jobs = []
    for c0 in range(0, D_QK, PROJ_COLS):
        def q_job(c0=c0):
            q_ref[:, c0:c0 + PROJ_COLS] = (proj(OFF_Q + c0, OFF_Q + c0 + PROJ_COLS)
                                           * (LOG2_E * ATTN_HEAD_DIM ** -0.5)).astype(BF16)
        jobs.append(q_job)
    for c0 in range(0, D_QK, PROJ_COLS):
        def k_job(c0=c0):
            k_ref[:, c0:c0 + PROJ_COLS] = proj(OFF_K + c0, OFF_K + c0 + PROJ_COLS).astype(BF16)
        jobs.append(k_job)
    for h0 in range(0, ATTN_HEADS, PROJ_COLS // HEAD_LANES):
        def v_job(h0=h0):
            h1 = h0 + PROJ_COLS // HEAD_LANES
            vt = lax.dot_general(wvt_ref[h0 * HEAD_LANES:h1 * HEAD_LANES, :], hb, (((1,), (1,)), ((), ())),
                                 preferred_element_type=F32)
            vt_ref[h0:h1] = vt.astype(BF16).reshape(h1 - h0, VT_ROWS, INPROJ_ROWS)
        jobs.append(v_job)
    for c0 in range(0, N_BRANCH * D_MODEL, PROJ_COLS):
        def g_job(c0=c0):
            gl_ref[:, c0:c0 + PROJ_COLS] = proj(OFF_GATE + c0, OFF_GATE + c0 + PROJ_COLS)
        jobs.append(g_job)

    y = jnp.zeros((tm, D_CONV), F32) + dwb_ref[...]
    done = 0
    for n, job in enumerate(jobs):
        job()
        upto = min(CONV_WIDTH, CONV_WIDTH * (n + 1) // CONV_SPREAD)
        if upto > done:
            y = conv_taps(range(done, upto), y)
            done = upto
            if done == CONV_WIDTH:
                mu = jnp.mean(y, axis=-1, keepdims=True)
                yc = y - mu
                var = jnp.mean(yc * yc, axis=-1, keepdims=True)
                yn = yc * lax.rsqrt(var + LN_EPS) * lng_ref[...] + lnb_ref[...]
                yb_ref[...] = jax.nn.silu(yn).astype(BF16)


def _inproj(x2, g, w, wvt, dw, dw_b, ln_g, ln_b, n_batch, length):
    n = x2.shape[0]
    tm = INPROJ_ROWS
    tiles = length // tm
    row = lambda width: pl.BlockSpec((tm, width), lambda i: (i, 0))
    dw_pad = jnp.zeros((CONV_HALO, D_CONV), F32).at[:CONV_WIDTH].set(dw)
    return pl.pallas_call(
        functools.partial(_inproj_kernel, tiles_per_seq=tiles),
        grid=(n // tm,),
        in_specs=[row(D_MODEL), _resident((1, D_MODEL)), _resident((D_MODEL, D_IN)),
                  _resident((D_V, D_MODEL)), _resident((CONV_HALO, D_CONV)), _resident((1, D_CONV)),
                  _resident((1, D_CONV)), _resident((1, D_CONV))],
        out_specs=[row(D_SSM), row(D_CONV), row(D_QK), row(D_QK),
                   pl.BlockSpec((None, ATTN_HEADS, None, VT_ROWS, tm),
                                lambda i: (i // tiles, 0, i % tiles, 0, 0)),
                   row(N_BRANCH * D_MODEL)],
        out_shape=[
            jax.ShapeDtypeStruct((n, D_SSM), BF16),
            jax.ShapeDtypeStruct((n, D_CONV), BF16),
            jax.ShapeDtypeStruct((n, D_QK), BF16),
            jax.ShapeDtypeStruct((n, D_QK), BF16),
            jax.ShapeDtypeStruct((n_batch, ATTN_HEADS, tiles, VT_ROWS, tm), BF16),
            jax.ShapeDtypeStruct((n, N_BRANCH * D_MODEL), F32),
        ],
        scratch_shapes=[pltpu.VMEM((CONV_HALO, D_CONV), F32),
                        pltpu.VMEM((CONV_HALO + tm, D_CONV), F32),
                        pltpu.VMEM((SUBLANES - 1, CONV_HALO + tm - SUBLANES, D_CONV), F32),
                        pltpu.VMEM((tm, D_CONV), F32)],
        compiler_params=_params(("arbitrary",)),
        name="inproj",
    )(x2, g, w, wvt, dw_pad, dw_b, ln_g, ln_b)


def _ssm_matrices(a_re, a_im, log_dt, b_re, b_im, c_re, c_im):
    hp = lax.Precision.HIGHEST
    t_len = SSM_CHUNK
    dt = jnp.exp(log_dt)[:, None]
    zr, zi = a_re * dt, a_im * dt
    er = jnp.exp(zr)
    lbr, lbi = er * jnp.cos(zi), er * jnp.sin(zi)
    den = a_re * a_re + a_im * a_im
    nr, ni = lbr - 1.0, lbi
    cr, ci = (nr * a_re + ni * a_im) / den, (ni * a_re - nr * a_im) / den
    bbr = cr[..., None] * b_re - ci[..., None] * b_im
    bbi = cr[..., None] * b_im + ci[..., None] * b_re
    tau = jnp.arange(t_len + 1, dtype=F32)[:, None, None]
    pe = jnp.exp(tau * zr)
    pwr, pwi = pe * jnp.cos(tau * zi), pe * jnp.sin(tau * zi)
    e_r = pwr[:t_len, :, :, None] * bbr - pwi[:t_len, :, :, None] * bbi
    e_i = pwr[:t_len, :, :, None] * bbi + pwi[:t_len, :, :, None] * bbr
    kern = (jnp.einsum("gcp,tgpd->tgcd", c_re, e_r, precision=hp)
            - jnp.einsum("gcp,tgpd->tgcd", c_im, e_i, precision=hp))
    lag = jnp.arange(t_len)[None, :] - jnp.arange(t_len)[:, None]
    pick = (lag[:, :, None] == jnp.arange(t_len)[None, None, :]).astype(BF16)
    gc = SSM_GROUP * t_len
    w_intra = jnp.einsum("stu,ugcd->gdsct", pick, kern.astype(BF16),
                         preferred_element_type=F32).astype(BF16).reshape(SSM_GROUPS, gc, gc)
    w_sr = e_r[::-1].transpose(1, 3, 0, 2).reshape(SSM_GROUPS, gc, SSM_STATE)
    w_si = e_i[::-1].transpose(1, 3, 0, 2).reshape(SSM_GROUPS, gc, SSM_STATE)
    p1r, p1i = pwr[1:, :, None, :], pwi[1:, :, None, :]
    w_or = (c_re * p1r - c_im * p1i).transpose(1, 3, 2, 0).reshape(SSM_GROUPS, SSM_STATE, gc)
    w_oi = (-(c_re * p1i + c_im * p1r)).transpose(1, 3, 2, 0).reshape(SSM_GROUPS, SSM_STATE, gc)
    a_r = pwr[t_len][:, None, :]
    a_i = pwi[t_len][:, None, :]
    return (w_intra, w_sr.astype(BF16), w_si.astype(BF16),
            w_or.astype(BF16), w_oi.astype(BF16), a_r, a_i)


def _ssm_kernel(u_ref, wi_ref, wsr_ref, wsi_ref, wor_ref, woi_ref, ar_ref, ai_ref, d_ref,
                o_ref, sr_ref, si_ref, xr_ref, xi_ref, *, n_batch, n_chunk):
    ub = u_ref[...]
    sr_ref[...] = jnp.dot(ub, wsr_ref[0], preferred_element_type=F32)
    si_ref[...] = jnp.dot(ub, wsi_ref[0], preferred_element_type=F32)
    a_r, a_i = ar_ref[0], ai_ref[0]

    def step(c, carry):
        nxt = []
        for b in range(n_batch):
            x_r, x_i = carry[2 * b], carry[2 * b + 1]
            row = pl.ds(b * n_chunk + c, 1)
            xr_ref[row, :] = x_r
            xi_ref[row, :] = x_i
            nxt.append(a_r * x_r - a_i * x_i + sr_ref[row, :])
            nxt.append(a_r * x_i + a_i * x_r + si_ref[row, :])
        return tuple(nxt)

    zero = jnp.zeros((1, SSM_STATE), F32)
    lax.fori_loop(0, n_chunk, step, (zero,) * (2 * n_batch))
    y = (jnp.dot(ub, wi_ref[0], preferred_element_type=F32)
         + jnp.dot(xr_ref[...].astype(BF16), wor_ref[0], preferred_element_type=F32)
         + jnp.dot(xi_ref[...].astype(BF16), woi_ref[0], preferred_element_type=F32)
         + d_ref[0] * ub.astype(F32))
    o_ref[...] = jax.nn.gelu(y, approximate=True).astype(BF16)


def _ssm(u2, mats, d_skip, n_batch, length):
    t_len = SSM_CHUNK
    n_chunk = length // t_len
    rows = n_batch * n_chunk
    gc = SSM_GROUP * t_len
    ug = u2.reshape(rows, t_len, D_SSM).transpose(0, 2, 1).reshape(rows, D_SSM * t_len)
    d_t = jnp.repeat(d_skip, t_len, axis=1).reshape(SSM_GROUPS, 1, gc)
    w_intra, w_sr, w_si, w_or, w_oi, a_r, a_i = mats
    grp = lambda *shape: pl.BlockSpec((1,) + shape, lambda g: (g, 0, 0))
    cols = pl.BlockSpec((rows, gc), lambda g: (0, g))
    yg = pl.pallas_call(
        functools.partial(_ssm_kernel, n_batch=n_batch, n_chunk=n_chunk),
        grid=(SSM_GROUPS,),
        in_specs=[cols, grp(gc, gc), grp(gc, SSM_STATE), grp(gc, SSM_STATE),
                  grp(SSM_STATE, gc), grp(SSM_STATE, gc), grp(1, SSM_STATE), grp(1, SSM_STATE),
                  grp(1, gc)],
        out_specs=cols,
        out_shape=jax.ShapeDtypeStruct((rows, D_SSM * t_len), BF16),
        scratch_shapes=[pltpu.VMEM((rows, SSM_STATE), F32)] * 4,
        compiler_params=_params(("parallel",)),
        name="ssm",
    )(ug, w_intra, w_sr, w_si, w_or, w_oi, a_r, a_i, d_t)
    return yg.reshape(rows, D_SSM, t_len).transpose(0, 2, 1).reshape(n_batch * length, D_SSM)


def _rel_bucket(rel):
    n = jnp.maximum(rel, 0)
    max_exact = REL_BUCKETS // 2
    nf = jnp.maximum(n, max_exact).astype(F32)
    large = max_exact + (jnp.log(nf / max_exact) / math.log(REL_MAX_DIST / max_exact)
                         * (REL_BUCKETS - max_exact)).astype(jnp.int32)
    large = jnp.minimum(large, REL_BUCKETS - 1)
    return jnp.where(n < max_exact, n, large)


def _attn_bias(rel_bias):
    blk = ATTN_BLOCK
    assert blk + 1 >= REL_MAX_DIST
    rb = rel_bias.astype(F32) * LOG2_E
    rel_diag = jnp.arange(blk)[None, :] - jnp.arange(blk)[:, None]
    rel = jnp.stack([rel_diag + blk, rel_diag])
    pick = (_rel_bucket(rel)[..., None] == jnp.arange(REL_BUCKETS)).astype(F32)
    near = jnp.einsum("kjib,bhn->hkjni", pick, rb.reshape(REL_BUCKETS, ATTN_HEADS, 2),
                      precision=lax.Precision.HIGHEST)
    near = jnp.where((rel >= 0)[None, :, :, None, :], near, -jnp.inf)
    near = near.reshape(ATTN_HEADS, 2, blk, 2 * blk)
    far = rb[_rel_bucket(jnp.full((), blk + 1, jnp.int32))]
    far = jnp.broadcast_to(far.reshape(ATTN_HEADS, 2, 1), (ATTN_HEADS, 2, blk))
    return near, far.reshape(ATTN_HEADS, 1, 2 * blk)


def _attn_kernel(sc_ref, q_ref, k_ref, vt_ref, near_ref, far_ref, g_ref, o_ref,
                 qs_ref, sa_ref, sb_ref, m_ref, l_ref, acc_ref):
    blk = ATTN_BLOCK
    n_sub = blk // INPROJ_ROWS
    heads = range(ATTN_HEAD_GROUP)
    i = pl.program_id(2)
    lane = lax.broadcasted_iota(jnp.int32, (blk, HEAD_LANES), 1)
    zero = jnp.zeros((blk, HEAD_LANES), BF16)
    for h in heads:
        q = q_ref[:, h * HEAD_LANES:(h + 1) * HEAD_LANES]
        qs_ref[h, 0:blk, :] = jnp.where(lane < ATTN_HEAD_DIM, q, zero)
        qs_ref[h, blk:, :] = jnp.where(lane >= ATTN_HEAD_DIM, q, zero)

    def scores(j, dst_ref):
        rows = pl.ds(pl.multiple_of(jnp.maximum(j, 0) * blk, blk), blk)
        for h in heads:
            kb = k_ref[rows, h * HEAD_LANES:(h + 1) * HEAD_LANES]
            dst_ref[h] = lax.dot_general(kb, qs_ref[h], (((1,), (1,)), ((), ())),
                                         preferred_element_type=F32)

    def v_tile(h, j):
        return jnp.concatenate([vt_ref[h, j * n_sub + r] for r in range(n_sub)], axis=1)

    def first_step(j, src_ref, kind):
        for h in heads:
            s = src_ref[h] + near_ref[h, kind]
            m_new = jnp.max(s, axis=0, keepdims=True)
            p = jnp.exp2(s - m_new)
            l_ref[h] = jnp.sum(p, axis=0, keepdims=True)
            acc_ref[h] = jnp.dot(v_tile(h, j), p.astype(BF16), preferred_element_type=F32)
            m_ref[h] = m_new

    def softmax_pv(j, src_ref, kind=None):
        for h in heads:
            s = src_ref[h]
            if kind is not None:
                s = s + near_ref[h, kind]
            cm = jnp.max(s, axis=0, keepdims=True)
            if kind is None:
                cm = cm + far_ref[h]
            m_old = m_ref[h]
            m_new = jnp.maximum(m_old, cm)
            shift = m_new if kind is not None else m_new - far_ref[h]
            p = jnp.exp2(s - shift)
            alpha = jnp.exp2(m_old - m_new)
            l_ref[h] = alpha * l_ref[h] + jnp.sum(p, axis=0, keepdims=True)
            acc_ref[h] = alpha * acc_ref[h] + jnp.dot(v_tile(h, j), p.astype(BF16),
                                                      preferred_element_type=F32)
            m_ref[h] = m_new

    scores(i, sa_ref)
    scores(i - 1, sb_ref)
    first_step(i, sa_ref, 1)

    @pl.when(i >= 1)
    def _():
        scores(i - 2, sa_ref)
        softmax_pv(i - 1, sb_ref, kind=0)

    n_far = jnp.maximum(i - 1, 0)

    def far_pair(j):
        scores(j - 1, sb_ref)
        softmax_pv(j, sa_ref)
        scores(j - 2, sa_ref)
        softmax_pv(j - 1, sb_ref)

    def far_quad(t, carry):
        far_pair(i - 2 - 4 * t)
        far_pair(i - 4 - 4 * t)
        return carry

    n_quad = n_far // 4
    lax.fori_loop(0, n_quad, far_quad, 0)

    @pl.when(n_far % 4 >= 2)
    def _():
        far_pair(i - 2 - 4 * n_quad)

    @pl.when(n_far % 2 == 1)
    def _():
        softmax_pv(0, sa_ref)

    lam, out_scale = sc_ref[0], sc_ref[1]
    for h in heads:
        a = acc_ref[h] * (1.0 / l_ref[h])
        o = a[:, :blk] - lam * a[:, blk:]
        o = o * lax.rsqrt(jnp.mean(o * o, axis=0, keepdims=True) + SUBLN_EPS) * g_ref[...] * out_scale
        o_ref[:, h * HEAD_LANES:(h + 1) * HEAD_LANES] = o.T.astype(BF16)


def _attention(q2, k2, vt, near, far, scalars, subln_g, n_batch, length):
    blk = ATTN_BLOCK
    nq = length // blk
    n = n_batch * length
    hg = ATTN_HEAD_GROUP
    return pl.pallas_call(
        _attn_kernel,
        grid=(n_batch, ATTN_HEADS // hg, nq),
        in_specs=[
            pl.BlockSpec(memory_space=pltpu.SMEM),
            pl.BlockSpec((blk, hg * HEAD_LANES), lambda b, h, i: (b * nq + i, h)),
            pl.BlockSpec((length, hg * HEAD_LANES), lambda b, h, i: (b, h)),
            pl.BlockSpec((None, hg, length // INPROJ_ROWS, VT_ROWS, INPROJ_ROWS),
                         lambda b, h, i: (b, h, 0, 0, 0)),
            pl.BlockSpec((hg, 2, blk, 2 * blk), lambda b, h, i: (h, 0, 0, 0)),
            pl.BlockSpec((hg, 1, 2 * blk), lambda b, h, i: (h, 0, 0)),
            pl.BlockSpec((HEAD_LANES, 1), lambda b, h, i: (0, 0)),
        ],
        out_specs=pl.BlockSpec((blk, hg * HEAD_LANES), lambda b, h, i: (b * nq + i, h)),
        out_shape=jax.ShapeDtypeStruct((n, D_V), BF16),
        scratch_shapes=[
            pltpu.VMEM((hg, 2 * blk, HEAD_LANES), BF16),
            pltpu.VMEM((hg, blk, 2 * blk), F32),
            pltpu.VMEM((hg, blk, 2 * blk), F32),
            pltpu.VMEM((hg, 1, 2 * blk), F32),
            pltpu.VMEM((hg, 1, 2 * blk), F32),
            pltpu.VMEM((hg, VT_ROWS, 2 * blk), F32),
        ],
        compiler_params=_params(("parallel", "parallel", "arbitrary")),
        name="diff_attention",
    )(scalars, q2, k2, vt, near, far, subln_g.reshape(HEAD_LANES, 1))


def _merge_kernel(x_ref, ys_ref, yb_ref, ao_ref, gl_ref, wglu_ref, wso_ref, wco_ref, wao_ref, wout_ref,
                  pg_ref, o_ref):
    z = jnp.dot(ys_ref[...], wglu_ref[...], preferred_element_type=F32)
    glu = (z[:, :D_SSM] * jax.nn.sigmoid(z[:, D_SSM:])).astype(BF16)
    y_a = jnp.dot(glu, wso_ref[...], preferred_element_type=F32)
    m = jax.nn.sigmoid(gl_ref[:, 0:D_MODEL]) * y_a
    y_b = jnp.dot(yb_ref[...], wco_ref[...], preferred_element_type=F32)
    m = m + jax.nn.sigmoid(gl_ref[:, D_MODEL:2 * D_MODEL]) * y_b
    y_c = jnp.dot(ao_ref[...], wao_ref[...], preferred_element_type=F32)
    m = m + jax.nn.sigmoid(gl_ref[:, 2 * D_MODEL:3 * D_MODEL]) * y_c
    r = jnp.dot(m.astype(BF16), wout_ref[...], preferred_element_type=F32)
    o_ref[...] = x_ref[...] + _rms(r, pg_ref[...])


def _merge(x2, ys, yb, ao, gl, w_glu, w_so, w_co, w_ao, w_out, post_g):
    n = x2.shape[0]
    tm = MERGE_ROWS
    row = lambda width: pl.BlockSpec((tm, width), lambda i: (i, 0))
    return pl.pallas_call(
        _merge_kernel,
        grid=(n // tm,),
        in_specs=[
            row(D_MODEL), row(D_SSM), row(D_CONV), row(D_V), row(N_BRANCH * D_MODEL),
            _resident((D_SSM, 2 * D_SSM)), _resident((D_SSM, D_MODEL)), _resident((D_CONV, D_MODEL)),
            _resident((D_V, D_MODEL)), _resident((D_MODEL, D_MODEL)), _resident((1, D_MODEL)),
        ],
        out_specs=row(D_MODEL),
        out_shape=jax.ShapeDtypeStruct((n, D_MODEL), F32),
        compiler_params=_params(("parallel",)),
        name="merge",
    )(x2, ys, yb, ao, gl, w_glu, w_so, w_co, w_ao, w_out, post_g)


def _ffn_kernel(x_ref, g1_ref, win_ref, wout_ref, g2_ref, o_ref):
    x = x_ref[...]
    hb = _rms(x, g1_ref[...]).astype(BF16)
    f = jnp.zeros(x.shape, F32)
    for c in range(D_FF // FFN_CHUNK):
        lo = c * FFN_CHUNK
        gate = jnp.dot(hb, win_ref[:, lo:lo + FFN_CHUNK], preferred_element_type=F32)
        up = jnp.dot(hb, win_ref[:, D_FF + lo:D_FF + lo + FFN_CHUNK], preferred_element_type=F32)
        act = (jax.nn.silu(gate) * up).astype(BF16)
        f = f + jnp.dot(act, wout_ref[lo:lo + FFN_CHUNK, :], preferred_element_type=F32)
    o_ref[...] = x + _rms(f, g2_ref[...])


def _ffn(x2, pre_g, w_in, w_out, post_g):
    n = x2.shape[0]
    tm = FFN_ROWS
    row = pl.BlockSpec((tm, D_MODEL), lambda i: (i, 0))
    return pl.pallas_call(
        _ffn_kernel,
        grid=(n // tm,),
        in_specs=[row, _resident((1, D_MODEL)), _resident((D_MODEL, 2 * D_FF)),
                  _resident((D_FF, D_MODEL)), _resident((1, D_MODEL))],
        out_specs=row,
        out_shape=jax.ShapeDtypeStruct((n, D_MODEL), F32),
        compiler_params=_params(("parallel",)),
        name="ffn",
    )(x2, pre_g, w_in, w_out, post_g)


def kernel(x, rel_bias, pre_mix_g, w_in, ssm_a_re, ssm_a_im, ssm_log_dt, ssm_b_re, ssm_b_im, ssm_c_re, ssm_c_im, ssm_d, w_ssm_glu, w_ssm_out, conv_dw, conv_dw_b, conv_ln_g, conv_ln_b, w_conv_out, lambda_q1, lambda_k1, lambda_q2, lambda_k2, attn_subln_g, w_attn_out, w_out, post_mix_g, pre_ffn_g, w_ffn_in, w_ffn_out, post_ffn_g):
    n_batch, length, _ = x.shape
    depth = w_in.shape[0]
    assert length % max(ATTN_BLOCK, MERGE_ROWS, FFN_ROWS, INPROJ_ROWS, SSM_CHUNK) == 0
    x2 = x.astype(F32).reshape(n_batch * length, D_MODEL)
    near, far = _attn_bias(rel_bias)
    vec = lambda a: a.astype(F32).reshape(1, -1)
    w_in_b = w_in.astype(BF16)
    w_vt = jnp.swapaxes(w_in[:, :, OFF_V:OFF_GATE], 1, 2).astype(BF16)
    for layer in range(depth):
        lambda_init = 0.8 - 0.6 * math.exp(-0.3 * layer)
        lam = (jnp.exp(jnp.sum(lambda_q1[layer].astype(F32) * lambda_k1[layer].astype(F32)))
               - jnp.exp(jnp.sum(lambda_q2[layer].astype(F32) * lambda_k2[layer].astype(F32)))
               + lambda_init)
        scalars = jnp.stack([lam, jnp.asarray(1.0 - lambda_init, F32)]).astype(F32)
        u, yb, q, k, vt, gl = _inproj(x2, vec(pre_mix_g[layer]), w_in_b[layer], w_vt[layer],
                                      conv_dw[layer].astype(F32), vec(conv_dw_b[layer]),
                                      vec(conv_ln_g[layer]), vec(conv_ln_b[layer]), n_batch, length)
        mats = _ssm_matrices(ssm_a_re[layer].astype(F32), ssm_a_im[layer].astype(F32),
                             ssm_log_dt[layer].astype(F32), ssm_b_re[layer].astype(F32),
                             ssm_b_im[layer].astype(F32), ssm_c_re[layer].astype(F32),
                             ssm_c_im[layer].astype(F32))
        ys = _ssm(u, mats, ssm_d[layer].astype(F32), n_batch, length)
        ao = _attention(q, k, vt, near, far, scalars, attn_subln_g[layer].astype(F32), n_batch, length)
        x2 = _merge(x2, ys, yb, ao, gl, w_ssm_glu[layer].astype(BF16), w_ssm_out[layer].astype(BF16),
                    w_conv_out[layer].astype(BF16), w_attn_out[layer].astype(BF16),
                    w_out[layer].astype(BF16), vec(post_mix_g[layer]))
        x2 = _ffn(x2, vec(pre_ffn_g[layer]), w_ffn_in[layer].astype(BF16), w_ffn_out[layer].astype(BF16),
                  vec(post_ffn_g[layer]))
    return x2.reshape(n_batch, length, D_MODEL).astype(x.dtype)
```

```python
import functools
import math

import jax
import jax.numpy as jnp
from jax import lax
from jax.experimental import pallas as pl
from jax.experimental.pallas import tpu as pltpu

F32 = jnp.float32
BF16 = jnp.bfloat16

D_MODEL = 1024
SSM_GROUP = 16
D_SSM = D_MODEL // 2
SSM_GROUPS = D_SSM // SSM_GROUP
SSM_STATE = 64
D_CONV = D_MODEL // 2
CONV_WIDTH = 31
ATTN_HEADS = 8
ATTN_HEAD_DIM = 64
ATTN_MAPS = 2 * ATTN_HEADS
D_QK = ATTN_MAPS * ATTN_HEAD_DIM
D_V = ATTN_HEADS * 2 * ATTN_HEAD_DIM
REL_BUCKETS = 32
REL_MAX_DIST = 128
D_FF = 2816
N_BRANCH = 3
OFF_SSM = 0
OFF_CONV = OFF_SSM + D_SSM
OFF_Q = OFF_CONV + 2 * D_CONV
OFF_K = OFF_Q + D_QK
OFF_V = OFF_K + D_QK
OFF_GATE = OFF_V + D_V
D_IN = OFF_GATE + N_BRANCH * D_MODEL
RMS_EPS = 1e-6
SUBLN_EPS = 1e-5
LN_EPS = 1e-5
LOG2_E = math.log2(math.e)

V7X_VMEM_LIMIT_BYTES = 56 * 1024 * 1024
SUBLANES = 8
HEAD_LANES = 2 * ATTN_HEAD_DIM

SSM_CHUNK = 32
CONV_HALO = 32
ATTN_BLOCK = 512
ATTN_HEAD_GROUP = 2
VT_ROWS = HEAD_LANES
INPROJ_ROWS = 256
MERGE_ROWS = 256
FFN_ROWS = 512
FFN_CHUNK = 256


def _params(sem):
    return pltpu.CompilerParams(dimension_semantics=sem, vmem_limit_bytes=V7X_VMEM_LIMIT_BYTES)


def _resident(shape):
    nd = len(shape)
    return pl.BlockSpec(shape, lambda *_: (0,) * nd, pipeline_mode=pl.Buffered(1))


def _rms(x, g):
    return x * lax.rsqrt(jnp.mean(x * x, axis=-1, keepdims=True) + RMS_EPS) * g


def _inproj_kernel(x_ref, g_ref, w_ref, wvt_ref, u_ref, c_ref, q_ref, k_ref, vt_ref, gl_ref):
    hb = _rms(x_ref[...], g_ref[...]).astype(BF16)

    def proj(lo, hi):
        return jnp.dot(hb, w_ref[:, lo:hi], preferred_element_type=F32)

    u_ref[...] = proj(OFF_SSM, OFF_CONV).astype(BF16)
    c_ref[...] = proj(OFF_CONV, OFF_Q)
    q_ref[...] = (proj(OFF_Q, OFF_K) * (LOG2_E * ATTN_HEAD_DIM ** -0.5)).astype(BF16)
    k_ref[...] = proj(OFF_K, OFF_V).astype(BF16)
    vt = lax.dot_general(wvt_ref[...], hb, (((1,), (1,)), ((), ())), preferred_element_type=F32)
    vt_ref[...] = vt.astype(BF16).reshape(ATTN_HEADS, VT_ROWS, INPROJ_ROWS)
    for b in range(N_BRANCH):
        lo = OFF_GATE + b * D_MODEL
        gl_ref[:, b * D_MODEL:(b + 1) * D_MODEL] = proj(lo, lo + D_MODEL)


def _inproj(x2, g, w, wvt, n_batch, length):
    n = x2.shape[0]
    tm = INPROJ_ROWS
    tiles = length // tm
    row = lambda width: pl.BlockSpec((tm, width), lambda i: (i, 0))
    return pl.pallas_call(
        _inproj_kernel,
        grid=(n // tm,),
        in_specs=[row(D_MODEL), _resident((1, D_MODEL)), _resident((D_MODEL, D_IN)),
                  _resident((D_V, D_MODEL))],
        out_specs=[row(D_SSM), row(2 * D_CONV), row(D_QK), row(D_QK),
                   pl.BlockSpec((None, ATTN_HEADS, None, VT_ROWS, tm),
                                lambda i: (i // tiles, 0, i % tiles, 0, 0)),
                   row(N_BRANCH * D_MODEL)],
        out_shape=[
            jax.ShapeDtypeStruct((n, D_SSM), BF16),
            jax.ShapeDtypeStruct((n, 2 * D_CONV), F32),
            jax.ShapeDtypeStruct((n, D_QK), BF16),
            jax.ShapeDtypeStruct((n, D_QK), BF16),
            jax.ShapeDtypeStruct((n_batch, ATTN_HEADS, tiles, VT_ROWS, tm), BF16),
            jax.ShapeDtypeStruct((n, N_BRANCH * D_MODEL), F32),
        ],
        compiler_params=_params(("parallel",)),
        name="inproj",
    )(x2, g, w, wvt)


def _ssm_matrices(a_re, a_im, log_dt, b_re, b_im, c_re, c_im):
    hp = lax.Precision.HIGHEST
    t_len = SSM_CHUNK
    dt = jnp.exp(log_dt)[:, None]
    zr, zi = a_re * dt, a_im * dt
    er = jnp.exp(zr)
    lbr, lbi = er * jnp.cos(zi), er * jnp.sin(zi)
    den = a_re * a_re + a_im * a_im
    nr, ni = lbr - 1.0, lbi
    cr, ci = (nr * a_re + ni * a_im) / den, (ni * a_re - nr * a_im) / den
    bbr = cr[..., None] * b_re - ci[..., None] * b_im
    bbi = cr[..., None] * b_im + ci[..., None] * b_re
    tau = jnp.arange(t_len + 1, dtype=F32)[:, None, None]
    pe = jnp.exp(tau * zr)
    pwr, pwi = pe * jnp.cos(tau * zi), pe * jnp.sin(tau * zi)
    e_r = pwr[:t_len, :, :, None] * bbr - pwi[:t_len, :, :, None] * bbi
    e_i = pwr[:t_len, :, :, None] * bbi + pwi[:t_len, :, :, None] * bbr
    kern = (jnp.einsum("gcp,tgpd->tgcd", c_re, e_r, precision=hp)
            - jnp.einsum("gcp,tgpd->tgcd", c_im, e_i, precision=hp))
    lag = jnp.arange(t_len)[None, :] - jnp.arange(t_len)[:, None]
    pick = (lag[:, :, None] == jnp.arange(t_len)[None, None, :]).astype(BF16)
    gc = SSM_GROUP * t_len
    w_intra = jnp.einsum("stu,ugcd->gdsct", pick, kern.astype(BF16),
                         preferred_element_type=F32).astype(BF16).reshape(SSM_GROUPS, gc, gc)
    w_sr = e_r[::-1].transpose(1, 3, 0, 2).reshape(SSM_GROUPS, gc, SSM_STATE)
    w_si = e_i[::-1].transpose(1, 3, 0, 2).reshape(SSM_GROUPS, gc, SSM_STATE)
    p1r, p1i = pwr[1:, :, None, :], pwi[1:, :, None, :]
    w_or = (c_re * p1r - c_im * p1i).transpose(1, 3, 2, 0).reshape(SSM_GROUPS, SSM_STATE, gc)
    w_oi = (-(c_re * p1i + c_im * p1r)).transpose(1, 3, 2, 0).reshape(SSM_GROUPS, SSM_STATE, gc)
    a_r = pwr[t_len][:, None, :]
    a_i = pwi[t_len][:, None, :]
    return (w_intra, w_sr.astype(BF16), w_si.astype(BF16),
            w_or.astype(BF16), w_oi.astype(BF16), a_r, a_i)


def _ssm_kernel(u_ref, wi_ref, wsr_ref, wsi_ref, wor_ref, woi_ref, ar_ref, ai_ref, d_ref,
                o_ref, sr_ref, si_ref, xr_ref, xi_ref, *, n_batch, n_chunk):
    ub = u_ref[...]
    sr_ref[...] = jnp.dot(ub, wsr_ref[0], preferred_element_type=F32)
    si_ref[...] = jnp.dot(ub, wsi_ref[0], preferred_element_type=F32)
    a_r, a_i = ar_ref[0], ai_ref[0]

    def step(c, carry):
        nxt = []
        for b in range(n_batch):
            x_r, x_i = carry[2 * b], carry[2 * b + 1]
            row = pl.ds(b * n_chunk + c, 1)
            xr_ref[row, :] = x_r
            xi_ref[row, :] = x_i
            nxt.append(a_r * x_r - a_i * x_i + sr_ref[row, :])
            nxt.append(a_r * x_i + a_i * x_r + si_ref[row, :])
        return tuple(nxt)

    zero = jnp.zeros((1, SSM_STATE), F32)
    lax.fori_loop(0, n_chunk, step, (zero,) * (2 * n_batch))
    y = (jnp.dot(ub, wi_ref[0], preferred_element_type=F32)
         + jnp.dot(xr_ref[...].astype(BF16), wor_ref[0], preferred_element_type=F32)
         + jnp.dot(xi_ref[...].astype(BF16), woi_ref[0], preferred_element_type=F32)
         + d_ref[0] * ub.astype(F32))
    o_ref[...] = jax.nn.gelu(y, approximate=True).astype(BF16)


def _ssm(u2, mats, d_skip, n_batch, length):
    t_len = SSM_CHUNK
    n_chunk = length // t_len
    rows = n_batch * n_chunk
    gc = SSM_GROUP * t_len
    ug = u2.reshape(rows, t_len, D_SSM).transpose(0, 2, 1).reshape(rows, D_SSM * t_len)
    d_t = jnp.repeat(d_skip, t_len, axis=1).reshape(SSM_GROUPS, 1, gc)
    w_intra, w_sr, w_si, w_or, w_oi, a_r, a_i = mats
    grp = lambda *shape: pl.BlockSpec((1,) + shape, lambda g: (g, 0, 0))
    cols = pl.BlockSpec((rows, gc), lambda g: (0, g))
    yg = pl.pallas_call(
        functools.partial(_ssm_kernel, n_batch=n_batch, n_chunk=n_chunk),
        grid=(SSM_GROUPS,),
        in_specs=[cols, grp(gc, gc), grp(gc, SSM_STATE), grp(gc, SSM_STATE),
                  grp(SSM_STATE, gc), grp(SSM_STATE, gc), grp(1, SSM_STATE), grp(1, SSM_STATE),
                  grp(1, gc)],
        out_specs=cols,
        out_shape=jax.ShapeDtypeStruct((rows, D_SSM * t_len), BF16),
        scratch_shapes=[pltpu.VMEM((rows, SSM_STATE), F32)] * 4,
        compiler_params=_params(("parallel",)),
        name="ssm",
    )(ug, w_intra, w_sr, w_si, w_or, w_oi, a_r, a_i, d_t)
    return yg.reshape(rows, D_SSM, t_len).transpose(0, 2, 1).reshape(n_batch * length, D_SSM)


def _rel_bucket(rel):
    n = jnp.maximum(rel, 0)
    max_exact = REL_BUCKETS // 2
    nf = jnp.maximum(n, max_exact).astype(F32)
    large = max_exact + (jnp.log(nf / max_exact) / math.log(REL_MAX_DIST / max_exact)
                         * (REL_BUCKETS - max_exact)).astype(jnp.int32)
    large = jnp.minimum(large, REL_BUCKETS - 1)
    return jnp.where(n < max_exact, n, large)


def _attn_bias(rel_bias):
    blk = ATTN_BLOCK
    assert blk + 1 >= REL_MAX_DIST
    rb = rel_bias.astype(F32) * LOG2_E
    rel_diag = jnp.arange(blk)[None, :] - jnp.arange(blk)[:, None]
    rel = jnp.stack([rel_diag + blk, rel_diag])
    pick = (_rel_bucket(rel)[..., None] == jnp.arange(REL_BUCKETS)).astype(F32)
    near = jnp.einsum("kjib,bhn->hkjni", pick, rb.reshape(REL_BUCKETS, ATTN_HEADS, 2),
                      precision=lax.Precision.HIGHEST)
    near = jnp.where((rel >= 0)[None, :, :, None, :], near, -jnp.inf)
    near = near.reshape(ATTN_HEADS, 2, blk, 2 * blk)
    far = rb[_rel_bucket(jnp.full((), blk + 1, jnp.int32))]
    far = jnp.broadcast_to(far.reshape(ATTN_HEADS, 2, 1), (ATTN_HEADS, 2, blk))
    return near, far.reshape(ATTN_HEADS, 1, 2 * blk)


def _attn_kernel(sc_ref, q_ref, k_ref, vt_ref, near_ref, far_ref, g_ref, o_ref,
                 qs_ref, sa_ref, sb_ref, m_ref, l_ref, acc_ref):
    blk = ATTN_BLOCK
    n_sub = blk // INPROJ_ROWS
    heads = range(ATTN_HEAD_GROUP)
    i = pl.program_id(2)
    lane = lax.broadcasted_iota(jnp.int32, (blk, HEAD_LANES), 1)
    zero = jnp.zeros((blk, HEAD_LANES), BF16)
    for h in heads:
        q = q_ref[:, h * HEAD_LANES:(h + 1) * HEAD_LANES]
        qs_ref[h, 0:blk, :] = jnp.where(lane < ATTN_HEAD_DIM, q, zero)
        qs_ref[h, blk:, :] = jnp.where(lane >= ATTN_HEAD_DIM, q, zero)

    def scores(j, dst_ref):
        rows = pl.ds(pl.multiple_of(jnp.maximum(j, 0) * blk, blk), blk)
        for h in heads:
            kb = k_ref[rows, h * HEAD_LANES:(h + 1) * HEAD_LANES]
            dst_ref[h] = lax.dot_general(kb, qs_ref[h], (((1,), (1,)), ((), ())),
                                         preferred_element_type=F32)

    def v_tile(h, j):
        return jnp.concatenate([vt_ref[h, j * n_sub + r] for r in range(n_sub)], axis=1)

    def first_step(j, src_ref, kind):
        for h in heads:
            s = src_ref[h] + near_ref[h, kind]
            m_new = jnp.max(s, axis=0, keepdims=True)
            p = jnp.exp2(s - m_new)
            l_ref[h] = jnp.sum(p, axis=0, keepdims=True)
            acc_ref[h] = jnp.dot(v_tile(h, j), p.astype(BF16), preferred_element_type=F32)
            m_ref[h] = m_new

    def softmax_pv(j, src_ref, kind=None):
        for h in heads:
            s = src_ref[h]
            if kind is not None:
                s = s + near_ref[h, kind]
            cm = jnp.max(s, axis=0, keepdims=True)
            if kind is None:
                cm = cm + far_ref[h]
            m_old = m_ref[h]
            m_new = jnp.maximum(m_old, cm)
            shift = m_new if kind is not None else m_new - far_ref[h]
            p = jnp.exp2(s - shift)
            alpha = jnp.exp2(m_old - m_new)
            l_ref[h] = alpha * l_ref[h] + jnp.sum(p, axis=0, keepdims=True)
            acc_ref[h] = alpha * acc_ref[h] + jnp.dot(v_tile(h, j), p.astype(BF16),
                                                      preferred_element_type=F32)
            m_ref[h] = m_new

    scores(i, sa_ref)
    scores(i - 1, sb_ref)
    first_step(i, sa_ref, 1)

    @pl.when(i >= 1)
    def _():
        scores(i - 2, sa_ref)
        softmax_pv(i - 1, sb_ref, kind=0)

    n_far = jnp.maximum(i - 1, 0)

    def far_pair(j):
        scores(j - 1, sb_ref)
        softmax_pv(j, sa_ref)
        scores(j - 2, sa_ref)
        softmax_pv(j - 1, sb_ref)

    def far_quad(t, carry):
        far_pair(i - 2 - 4 * t)
        far_pair(i - 4 - 4 * t)
        return carry

    n_quad = n_far // 4
    lax.fori_loop(0, n_quad, far_quad, 0)

    @pl.when(n_far % 4 >= 2)
    def _():
        far_pair(i - 2 - 4 * n_quad)

    @pl.when(n_far % 2 == 1)
    def _():
        softmax_pv(0, sa_ref)

    lam, out_scale = sc_ref[0], sc_ref[1]
    for h in heads:
        a = acc_ref[h] * (1.0 / l_ref[h])
        o = a[:, :blk] - lam * a[:, blk:]
        o = o * lax.rsqrt(jnp.mean(o * o, axis=0, keepdims=True) + SUBLN_EPS) * g_ref[...] * out_scale
        o_ref[:, h * HEAD_LANES:(h + 1) * HEAD_LANES] = o.T.astype(BF16)


def _attention(q2, k2, vt, near, far, scalars, subln_g, n_batch, length):
    blk = ATTN_BLOCK
    nq = length // blk
    n = n_batch * length
    hg = ATTN_HEAD_GROUP
    return pl.pallas_call(
        _attn_kernel,
        grid=(n_batch, ATTN_HEADS // hg, nq),
        in_specs=[
            pl.BlockSpec(memory_space=pltpu.SMEM),
            pl.BlockSpec((blk, hg * HEAD_LANES), lambda b, h, i: (b * nq + i, h)),
            pl.BlockSpec((length, hg * HEAD_LANES), lambda b, h, i: (b, h)),
            pl.BlockSpec((None, hg, length // INPROJ_ROWS, VT_ROWS, INPROJ_ROWS),
                         lambda b, h, i: (b, h, 0, 0, 0)),
            pl.BlockSpec((hg, 2, blk, 2 * blk), lambda b, h, i: (h, 0, 0, 0)),
            pl.BlockSpec((hg, 1, 2 * blk), lambda b, h, i: (h, 0, 0)),
            pl.BlockSpec((HEAD_LANES, 1), lambda b, h, i: (0, 0)),
        ],
        out_specs=pl.BlockSpec((blk, hg * HEAD_LANES), lambda b, h, i: (b * nq + i, h)),
        out_shape=jax.ShapeDtypeStruct((n, D_V), BF16),
        scratch_shapes=[
            pltpu.VMEM((hg, 2 * blk, HEAD_LANES), BF16),
            pltpu.VMEM((hg, blk, 2 * blk), F32),
            pltpu.VMEM((hg, blk, 2 * blk), F32),
            pltpu.VMEM((hg, 1, 2 * blk), F32),
            pltpu.VMEM((hg, 1, 2 * blk), F32),
            pltpu.VMEM((hg, VT_ROWS, 2 * blk), F32),
        ],
        compiler_params=_params(("parallel", "parallel", "arbitrary")),
        name="diff_attention",
    )(scalars, q2, k2, vt, near, far, subln_g.reshape(HEAD_LANES, 1))


def _merge_kernel(x_ref, ys_ref, cc_ref, cp_ref, ao_ref, gl_ref, wglu_ref, wso_ref, dw_ref, dwb_ref,
                  lng_ref, lnb_ref, wco_ref, wao_ref, wout_ref, pg_ref, o_ref, gbuf_ref, gsh_ref,
                  *, tiles_per_seq):
    tm = MERGE_ROWS
    i = pl.program_id(0)
    z = jnp.dot(ys_ref[...], wglu_ref[...], preferred_element_type=F32)
    glu = (z[:, :D_SSM] * jax.nn.sigmoid(z[:, D_SSM:])).astype(BF16)
    y_a = jnp.dot(glu, wso_ref[...], preferred_element_type=F32)
    m = jax.nn.sigmoid(gl_ref[:, 0:D_MODEL]) * y_a
    cc = cc_ref[...]
    cp = cp_ref[...]
    g_prev = cp[:, :D_CONV] * jax.nn.sigmoid(cp[:, D_CONV:])
    g_prev = jnp.where(i % tiles_per_seq == 0, jnp.zeros_like(g_prev), g_prev)
    gbuf_ref[0:CONV_HALO, :] = g_prev
    gbuf_ref[CONV_HALO:, :] = cc[:, :D_CONV] * jax.nn.sigmoid(cc[:, D_CONV:])
    shifted_rows = tm + CONV_HALO - SUBLANES
    for r in range(1, SUBLANES):
        gsh_ref[r - 1] = gbuf_ref[r:r + shifted_rows, :]
    y = jnp.zeros((tm, D_CONV), F32) + dwb_ref[...]
    base = CONV_HALO - (CONV_WIDTH - 1)
    for tap in range(CONV_WIDTH):
        r, lo = (base + tap) % SUBLANES, (base + tap) // SUBLANES * SUBLANES
        win = gbuf_ref[lo:lo + tm, :] if r == 0 else gsh_ref[r - 1, lo:lo + tm, :]
        y = y + win * dw_ref[tap:tap + 1, :]
    mu = jnp.mean(y, axis=-1, keepdims=True)
    yc = y - mu
    var = jnp.mean(yc * yc, axis=-1, keepdims=True)
    yn = yc * lax.rsqrt(var + LN_EPS) * lng_ref[...] + lnb_ref[...]
    y_b = jnp.dot(jax.nn.silu(yn).astype(BF16), wco_ref[...], preferred_element_type=F32)
    m = m + jax.nn.sigmoid(gl_ref[:, D_MODEL:2 * D_MODEL]) * y_b
    y_c = jnp.dot(ao_ref[...], wao_ref[...], preferred_element_type=F32)
    m = m + jax.nn.sigmoid(gl_ref[:, 2 * D_MODEL:3 * D_MODEL]) * y_c
    r = jnp.dot(m.astype(BF16), wout_ref[...], preferred_element_type=F32)
    o_ref[...] = x_ref[...] + _rms(r, pg_ref[...])


def _merge(x2, ys, c_in, ao, gl, w_glu, w_so, dw, dw_b, ln_g, ln_b, w_co, w_ao, w_out, post_g, length):
    n = x2.shape[0]
    tm = MERGE_ROWS
    halo_blocks = tm // CONV_HALO
    row = lambda width: pl.BlockSpec((tm, width), lambda i: (i, 0))
    dw_pad = jnp.zeros((CONV_HALO, D_CONV), F32).at[:CONV_WIDTH].set(dw)
    return pl.pallas_call(
        functools.partial(_merge_kernel, tiles_per_seq=length // tm),
        grid=(n // tm,),
        in_specs=[
            row(D_MODEL), row(D_SSM), row(2 * D_CONV),
            pl.BlockSpec((CONV_HALO, 2 * D_CONV), lambda i: (jnp.maximum(i * halo_blocks - 1, 0), 0)),
            row(D_V), row(N_BRANCH * D_MODEL),
            _resident((D_SSM, 2 * D_SSM)), _resident((D_SSM, D_MODEL)),
            _resident((CONV_HALO, D_CONV)), _resident((1, D_CONV)), _resident((1, D_CONV)),
            _resident((1, D_CONV)), _resident((D_CONV, D_MODEL)), _resident((D_V, D_MODEL)),
            _resident((D_MODEL, D_MODEL)), _resident((1, D_MODEL)),
        ],
        out_specs=row(D_MODEL),
        out_shape=jax.ShapeDtypeStruct((n, D_MODEL), F32),
        scratch_shapes=[pltpu.VMEM((CONV_HALO + tm, D_CONV), F32),
                        pltpu.VMEM((SUBLANES - 1, CONV_HALO + tm - SUBLANES, D_CONV), F32)],
        compiler_params=_params(("parallel",)),
        name="merge",
    )(x2, ys, c_in, c_in, ao, gl, w_glu, w_so, dw_pad, dw_b, ln_g, ln_b, w_co, w_ao, w_out, post_g)


def _ffn_kernel(x_ref, g1_ref, win_ref, wout_ref, g2_ref, o_ref):
    x = x_ref[...]
    hb = _rms(x, g1_ref[...]).astype(BF16)
    f = jnp.zeros(x.shape, F32)
    for c in range(D_FF // FFN_CHUNK):
        lo = c * FFN_CHUNK
        gate = jnp.dot(hb, win_ref[:, lo:lo + FFN_CHUNK], preferred_element_type=F32)
        up = jnp.dot(hb, win_ref[:, D_FF + lo:D_FF + lo + FFN_CHUNK], preferred_element_type=F32)
        act = (jax.nn.silu(gate) * up).astype(BF16)
        f = f + jnp.dot(act, wout_ref[lo:lo + FFN_CHUNK, :], preferred_element_type=F32)
    o_ref[...] = x + _rms(f, g2_ref[...])


def _ffn(x2, pre_g, w_in, w_out, post_g):
    n = x2.shape[0]
    tm = FFN_ROWS
    row = pl.BlockSpec((tm, D_MODEL), lambda i: (i, 0))
    return pl.pallas_call(
        _ffn_kernel,
        grid=(n // tm,),
        in_specs=[row, _resident((1, D_MODEL)), _resident((D_MODEL, 2 * D_FF)),
                  _resident((D_FF, D_MODEL)), _resident((1, D_MODEL))],
        out_specs=row,
        out_shape=jax.ShapeDtypeStruct((n, D_MODEL), F32),
        compiler_params=_params(("parallel",)),
        name="ffn",
    )(x2, pre_g, w_in, w_out, post_g)


def kernel(x, rel_bias, pre_mix_g, w_in, ssm_a_re, ssm_a_im, ssm_log_dt, ssm_b_re, ssm_b_im, ssm_c_re, ssm_c_im, ssm_d, w_ssm_glu, w_ssm_out, conv_dw, conv_dw_b, conv_ln_g, conv_ln_b, w_conv_out, lambda_q1, lambda_k1, lambda_q2, lambda_k2, attn_subln_g, w_attn_out, w_out, post_mix_g, pre_ffn_g, w_ffn_in, w_ffn_out, post_ffn_g):
    n_batch, length, _ = x.shape
    depth = w_in.shape[0]
    assert length % max(ATTN_BLOCK, MERGE_ROWS, FFN_ROWS, INPROJ_ROWS, SSM_CHUNK) == 0
    x2 = x.astype(F32).reshape(n_batch * length, D_MODEL)
    near, far = _attn_bias(rel_bias)
    vec = lambda a: a.astype(F32).reshape(1, -1)
    w_in_b = w_in.astype(BF16)
    w_vt = jnp.swapaxes(w_in[:, :, OFF_V:OFF_GATE], 1, 2).astype(BF16)
    for layer in range(depth):
        lambda_init = 0.8 - 0.6 * math.exp(-0.3 * layer)
        lam = (jnp.exp(jnp.sum(lambda_q1[layer].astype(F32) * lambda_k1[layer].astype(F32)))
               - jnp.exp(jnp.sum(lambda_q2[layer].astype(F32) * lambda_k2[layer].astype(F32)))
               + lambda_init)
        scalars = jnp.stack([lam, jnp.asarray(1.0 - lambda_init, F32)]).astype(F32)
        u, c_in, q, k, vt, gl = _inproj(x2, vec(pre_mix_g[layer]), w_in_b[layer], w_vt[layer], n_batch, length)
        mats = _ssm_matrices(ssm_a_re[layer].astype(F32), ssm_a_im[layer].astype(F32),
                             ssm_log_dt[layer].astype(F32), ssm_b_re[layer].astype(F32),
                             ssm_b_im[layer].astype(F32), ssm_c_re[layer].astype(F32),
                             ssm_c_im[layer].astype(F32))
        ys = _ssm(u, mats, ssm_d[layer].astype(F32), n_batch, length)
        ao = _attention(q, k, vt, near, far, scalars, attn_subln_g[layer].astype(F32), n_batch, length)
        x2 = _merge(x2, ys, c_in, ao, gl, w_ssm_glu[layer].astype(BF16), w_ssm_out[layer].astype(BF16),
                    conv_dw[layer].astype(F32), vec(conv_dw_b[layer]), vec(conv_ln_g[layer]),
                    vec(conv_ln_b[layer]), w_conv_out[layer].astype(BF16), w_attn_out[layer].astype(BF16),
                    w_out[layer].astype(BF16), vec(post_mix_g[layer]), length)
        x2 = _ffn(x2, vec(pre_ffn_g[layer]), w_ffn_in[layer].astype(BF16), w_ffn_out[layer].astype(BF16),
                  vec(post_ffn_g[layer]))
    return x2.reshape(n_batch, length, D_MODEL).astype(x.dtype)
```
